```python
import jax, jax.numpy as jnp
from jax import lax
import numpy as np

D_MODEL = 1024
BATCH = 2
SEQ = 16384
DEPTH = 4

HEAD_DIM = 64
A_Q_HEADS = 4
A_KV_HEADS = 2
A_WINDOW = 128
B_HEADS = 6
B_BRANCHES = ((128, 1), (512, 4), (2048, 16))
C_WIDTH = 384
C_BLOCKS = 6
C_CONV = 4
C_EXP = 8.0
D_FF = 2816
BLOCK = 128
ROPE_THETA = 10000.0
EPS = 1e-6
SCALE = HEAD_DIM ** -0.5

A_WIDTH = A_Q_HEADS * HEAD_DIM
A_KV_WIDTH = A_KV_HEADS * HEAD_DIM
B_WIDTH = B_HEADS * HEAD_DIM
MIX_WIDTH = A_WIDTH + B_WIDTH + C_WIDTH
IN_SPLIT_SIZES = (A_WIDTH, A_KV_WIDTH, A_KV_WIDTH, B_WIDTH, B_WIDTH, B_WIDTH, C_WIDTH, C_WIDTH)
IN_COLS = A_WIDTH + 2 * A_KV_WIDTH + 3 * B_WIDTH + 2 * C_WIDTH

kernel_name = "hymba_style_swa_dilated_rglru_macaron"


def rms_norm(x, g):
    xf = x.astype(jnp.float32)
    y = xf * lax.rsqrt(jnp.mean(xf * xf, axis=-1, keepdims=True) + EPS)
    return (y * g.astype(jnp.float32)).astype(x.dtype)


def swiglu(x, w_gate, w_up, w_down):
    return (jax.nn.silu(x @ w_gate) * (x @ w_up)) @ w_down


def rope_tables(positions):
    inv = 1.0 / (ROPE_THETA ** (jnp.arange(0, HEAD_DIM, 2, dtype=jnp.float32) / HEAD_DIM))
    ang = positions.astype(jnp.float32)[..., None] * inv
    return jnp.cos(ang), jnp.sin(ang)


def apply_rope(x, cos, sin):
    x1, x2 = jnp.split(x.astype(jnp.float32), 2, axis=-1)
    c = cos[:, :, None, :]
    s = sin[:, :, None, :]
    return jnp.concatenate([x1 * c - x2 * s, x2 * c + x1 * s], axis=-1).astype(x.dtype)


def banded_attention(q, k, v, max_dist):
    n, g, L, hd = q.shape
    n_prev = -(-max_dist // BLOCK)
    nb = -(-L // BLOCK)
    Lp = nb * BLOCK
    qb = jnp.pad(q, ((0, 0), (0, 0), (0, Lp - L), (0, 0))).reshape(n, g, nb, BLOCK, hd)
    pad = ((0, 0), (n_prev * BLOCK, Lp - L), (0, 0))
    kp = jnp.pad(k, pad)
    vp = jnp.pad(v, pad)
    kb = jnp.concatenate([kp[:, j * BLOCK:j * BLOCK + Lp].reshape(n, nb, BLOCK, hd) for j in range(n_prev + 1)], axis=2)
    vb = jnp.concatenate([vp[:, j * BLOCK:j * BLOCK + Lp].reshape(n, nb, BLOCK, hd) for j in range(n_prev + 1)], axis=2)
    q_pos = jnp.arange(Lp).reshape(nb, BLOCK, 1)
    k_pos = (jnp.arange(nb)[:, None] * BLOCK + jnp.arange((n_prev + 1) * BLOCK)[None, :] - n_prev * BLOCK)[:, None, :]
    dist = q_pos - k_pos
    mask = (dist >= 0) & (dist <= max_dist) & (k_pos >= 0)
    s = jnp.einsum('ngbqd,nbkd->ngbqk', qb, kb).astype(jnp.float32) * SCALE
    s = jnp.where(mask, s, -jnp.inf)
    m = jnp.max(s, axis=-1, keepdims=True)
    p = jnp.exp(s - m)
    l = jnp.sum(p, axis=-1, keepdims=True)
    o = jnp.einsum('ngbqk,nbkd->ngbqd', p.astype(v.dtype), vb).astype(jnp.float32) / l
    lse = (m + jnp.log(l))[..., 0]
    o = o.reshape(n, g, Lp, hd)[:, :, :L].astype(v.dtype)
    lse = lse.reshape(n, g, Lp)[:, :, :L]
    return o, lse


def swa_sink_mixer(q, k, v, sinks):
    b, s = q.shape[:2]
    g = A_Q_HEADS // A_KV_HEADS
    qh = q.reshape(b, s, A_KV_HEADS, g, HEAD_DIM).transpose(0, 2, 3, 1, 4).reshape(b * A_KV_HEADS, g, s, HEAD_DIM)
    kh = k.transpose(0, 2, 1, 3).reshape(b * A_KV_HEADS, s, HEAD_DIM)
    vh = v.transpose(0, 2, 1, 3).reshape(b * A_KV_HEADS, s, HEAD_DIM)
    o, lse = banded_attention(qh, kh, vh, A_WINDOW - 1)
    sink = jnp.tile(sinks.astype(jnp.float32).reshape(A_KV_HEADS, g), (b, 1))[:, :, None]
    o = o * jax.nn.sigmoid(lse - sink)[..., None].astype(o.dtype)
    return o.reshape(b, A_KV_HEADS, g, s, HEAD_DIM).transpose(0, 3, 1, 2, 4).reshape(b, s, A_WIDTH)


def dilated_mixer(q, k, v):
    b, s, h, hd = q.shape
    outs, lses = [], []
    for window, d in B_BRANCHES:
        def gather(t):
            return t.reshape(b, s // d, d, h, hd).transpose(0, 3, 2, 1, 4).reshape(b * h * d, s // d, hd)
        o, lse = banded_attention(gather(q)[:, None], gather(k), gather(v), window // d)
        outs.append(o[:, 0].reshape(b, h, d, s // d, hd).transpose(0, 3, 2, 1, 4).reshape(b, s, h, hd))
        lses.append(lse[:, 0].reshape(b, h, d, s // d).transpose(0, 3, 2, 1).reshape(b, s, h))
    w = jax.nn.softmax(jnp.stack(lses, axis=-1), axis=-1)
    o = outs[0] * w[..., 0:1].astype(q.dtype)
    for i in range(1, len(B_BRANCHES)):
        o = o + outs[i] * w[..., i:i + 1].astype(q.dtype)
    return o.reshape(b, s, B_WIDTH)


def _lru_combine(left, right):
    a1, b1 = left
    a2, b2 = right
    return a1 * a2, a2 * b1 + b2


def rglru_mixer(xc, gate, conv_w, conv_b, w_r, b_r, w_i, b_i, lam, positions):
    b, s, c = xc.shape
    xp = jnp.pad(xc, ((0, 0), (C_CONV - 1, 0), (0, 0)))
    y = conv_b + conv_w[0] * xp[:, C_CONV - 1:C_CONV - 1 + s]
    for j in range(1, C_CONV):
        y = y + conv_w[j] * xp[:, C_CONV - 1 - j:C_CONV - 1 - j + s]
    yb = y.reshape(b, s, C_BLOCKS, c // C_BLOCKS)
    r = jax.nn.sigmoid(jnp.einsum('bshi,hij->bshj', yb, w_r) + b_r).reshape(b, s, c)
    ig = jax.nn.sigmoid(jnp.einsum('bshi,hij->bshj', yb, w_i) + b_i).reshape(b, s, c)
    log_a = -C_EXP * r.astype(jnp.float32) * jax.nn.softplus(-lam.astype(jnp.float32))
    reset = (positions == 0)[..., None]
    a = jnp.where(reset, 0.0, jnp.exp(log_a))
    mult = jnp.where(reset, 1.0, jnp.sqrt(-jnp.expm1(2.0 * log_a)))
    bx = mult * (ig * y).astype(jnp.float32)
    _, hs = lax.associative_scan(_lru_combine, (a, bx), axis=1)
    return (hs * jax.nn.gelu(gate.astype(jnp.float32))).astype(xc.dtype)


def _split_cols(proj):
    out, start = [], 0
    for size in IN_SPLIT_SIZES:
        out.append(proj[..., start:start + size])
        start += size
    return out


def setup_inputs(seed: int = 0) -> dict:
    key = jax.random.key(seed)
    ks = jax.random.split(key, 24)
    f32 = jnp.float32
    L = DEPTH
    nrm = lambda k, shape, scale: jax.random.normal(k, shape, f32) * scale
    a0 = jax.random.uniform(ks[14], (L, C_WIDTH), f32, minval=0.9, maxval=0.999)
    return {
        "x": jax.random.normal(ks[0], (BATCH, SEQ, D_MODEL), f32),
        "positions": jnp.broadcast_to(jnp.arange(SEQ, dtype=jnp.int32), (BATCH, SEQ)),
        "norm_ffn1": 1.0 + nrm(ks[1], (L, D_MODEL), 0.02),
        "ffn1_gate": nrm(ks[2], (L, D_MODEL, D_FF), D_MODEL ** -0.5),
        "ffn1_up": nrm(ks[3], (L, D_MODEL, D_FF), D_MODEL ** -0.5),
        "ffn1_down": nrm(ks[4], (L, D_FF, D_MODEL), D_FF ** -0.5),
        "norm_mix": 1.0 + nrm(ks[5], (L, D_MODEL), 0.02),
        "w_in": nrm(ks[6], (L, D_MODEL, IN_COLS), D_MODEL ** -0.5),
        "attn_sinks": nrm(ks[7], (L, A_Q_HEADS), 1.0),
        "conv_w": nrm(ks[8], (L, C_CONV, C_WIDTH), C_CONV ** -0.5),
        "conv_b": nrm(ks[9], (L, C_WIDTH), 0.01),
        "rg_w_r": nrm(ks[10], (L, C_BLOCKS, C_WIDTH // C_BLOCKS, C_WIDTH // C_BLOCKS), (C_WIDTH // C_BLOCKS) ** -0.5),
        "rg_b_r": nrm(ks[11], (L, C_BLOCKS, C_WIDTH // C_BLOCKS), 0.01),
        "rg_w_i": nrm(ks[12], (L, C_BLOCKS, C_WIDTH // C_BLOCKS, C_WIDTH // C_BLOCKS), (C_WIDTH // C_BLOCKS) ** -0.5),
        "rg_b_i": nrm(ks[13], (L, C_BLOCKS, C_WIDTH // C_BLOCKS), 0.01),
        "rg_lambda": jnp.log(a0) - jnp.log1p(-a0),
        "w_out": nrm(ks[15], (L, MIX_WIDTH, D_MODEL), MIX_WIDTH ** -0.5),
        "norm_ffn2": 1.0 + nrm(ks[16], (L, D_MODEL), 0.02),
        "ffn2_gate": nrm(ks[17], (L, D_MODEL, D_FF), D_MODEL ** -0.5),
        "ffn2_up": nrm(ks[18], (L, D_MODEL, D_FF), D_MODEL ** -0.5),
        "ffn2_down": nrm(ks[19], (L, D_FF, D_MODEL), D_FF ** -0.5),
        "norm_final": 1.0 + nrm(ks[20], (D_MODEL,), 0.02),
    }


def reference(x, positions, norm_ffn1, ffn1_gate, ffn1_up, ffn1_down, norm_mix, w_in, attn_sinks,
              conv_w, conv_b, rg_w_r, rg_b_r, rg_w_i, rg_b_i, rg_lambda, w_out,
              norm_ffn2, ffn2_gate, ffn2_up, ffn2_down, norm_final):
    b, s, _ = x.shape
    cos, sin = rope_tables(positions)
    for l in range(DEPTH):
        x = x + 0.5 * swiglu(rms_norm(x, norm_ffn1[l]), ffn1_gate[l], ffn1_up[l], ffn1_down[l])
        h = rms_norm(x, norm_mix[l])
        qa, ka, va, qb, kb, vb, xc, gc = _split_cols(h @ w_in[l])
        qa = apply_rope(qa.reshape(b, s, A_Q_HEADS, HEAD_DIM), cos, sin)
        ka = apply_rope(ka.reshape(b, s, A_KV_HEADS, HEAD_DIM), cos, sin)
        va = va.reshape(b, s, A_KV_HEADS, HEAD_DIM)
        out_a = swa_sink_mixer(qa, ka, va, attn_sinks[l])
        qb = apply_rope(qb.reshape(b, s, B_HEADS, HEAD_DIM), cos, sin)
        kb = apply_rope(kb.reshape(b, s, B_HEADS, HEAD_DIM), cos, sin)
        vb = vb.reshape(b, s, B_HEADS, HEAD_DIM)
        out_b = dilated_mixer(qb, kb, vb)
        out_c = rglru_mixer(xc, gc, conv_w[l], conv_b[l], rg_w_r[l], rg_b_r[l], rg_w_i[l], rg_b_i[l],
                            rg_lambda[l], positions)
        x = x + jnp.concatenate([out_a, out_b, out_c], axis=-1) @ w_out[l]
        x = x + 0.5 * swiglu(rms_norm(x, norm_ffn2[l]), ffn2_gate[l], ffn2_up[l], ffn2_down[l])
    return rms_norm(x, norm_final)
```

```python
import functools

import jax
import jax.numpy as jnp
from jax import lax
from jax.experimental import pallas as pl
from jax.experimental.pallas import tpu as pltpu

D_MODEL = 1024
HEAD_DIM = 64
A_Q_HEADS = 4
A_KV_HEADS = 2
A_WINDOW = 128
B_HEADS = 6
B_BRANCHES = ((128, 1), (512, 4), (2048, 16))
C_WIDTH = 384
C_BLOCKS = 6
C_CONV = 4
C_EXP = 8.0
D_FF = 2816
BLOCK = 128
ROPE_THETA = 10000.0
EPS = 1e-6
SCALE = HEAD_DIM ** -0.5

A_WIDTH = A_Q_HEADS * HEAD_DIM
A_KV_WIDTH = A_KV_HEADS * HEAD_DIM
B_WIDTH = B_HEADS * HEAD_DIM
MIX_WIDTH = A_WIDTH + B_WIDTH + C_WIDTH
IN_COLS = A_WIDTH + 2 * A_KV_WIDTH + 3 * B_WIDTH + 2 * C_WIDTH

LANES = 128
SUBLANES = 8
FF_CHUNK = 256
N_FF_CHUNKS = D_FF // FF_CHUNK
B_SLABS = B_WIDTH // LANES
A_SLABS = A_WIDTH // LANES
MAX_DIL = max(d for _, d in B_BRANCHES)
B_TILE = BLOCK * MAX_DIL
VMEM_LIMIT = 56 * 1024 * 1024

F32 = jnp.float32
BF16 = jnp.bfloat16


def _params(sem):
    return pltpu.CompilerParams(dimension_semantics=sem, vmem_limit_bytes=VMEM_LIMIT)


def _resident(shape, index_map):
    return pl.BlockSpec(shape, index_map, pipeline_mode=pl.Buffered(1))


def _rms(x, g):
    return x * lax.rsqrt(jnp.mean(x * x, axis=-1, keepdims=True) + EPS) * g


def _tables_kernel(pos_ref, inv_ref, sign_ref, cos_ref, sin_ref, keep_ref):
    inv = inv_ref[...]
    sign = sign_ref[...]
    for r in range(pos_ref.shape[0]):
        p = pos_ref[r:r + 1, :].astype(F32)
        ang_t = inv * p
        rows = pl.ds(r * LANES, LANES)
        cos_ref[rows, :] = jnp.cos(ang_t).T
        sin_ref[rows, :] = (jnp.sin(ang_t) * sign).T
        keep_ref[rows, :] = jnp.where(jnp.broadcast_to(p, (LANES, LANES)) == 0.0, 0.0, 1.0).T


def _rope_tables(positions):
    n = positions.size
    rows = SUBLANES
    pos2 = positions.reshape(n // LANES, LANES)
    half = HEAD_DIM // 2
    inv = 1.0 / (ROPE_THETA ** (jnp.arange(0, HEAD_DIM, 2, dtype=F32) / HEAD_DIM))
    inv_col = jnp.tile(inv, LANES // half).reshape(LANES, 1)
    sign_col = jnp.tile(jnp.concatenate([-jnp.ones(half, F32), jnp.ones(half, F32)]),
                        LANES // HEAD_DIM).reshape(LANES, 1)
    tab = jax.ShapeDtypeStruct((n, LANES), F32)
    out_spec = pl.BlockSpec((rows * LANES, LANES), lambda i: (i, 0))
    return pl.pallas_call(
        _tables_kernel,
        grid=(n // (rows * LANES),),
        in_specs=[pl.BlockSpec((rows, LANES), lambda i: (i, 0)),
                  pl.BlockSpec((LANES, 1), lambda i: (0, 0)),
                  pl.BlockSpec((LANES, 1), lambda i: (0, 0))],
        out_specs=[out_spec, out_spec, out_spec],
        out_shape=[tab, tab, tab],
        compiler_params=_params(("arbitrary",)),
        name="rope_tables",
    )(pos2, inv_col, sign_col)


def _ffn_kernel(x_ref, g_ref, wg_ref, wu_ref, wd_ref, gf_ref, o_ref, xn_ref, acc_ref, *, final):
    x = x_ref[...]
    xn_ref[...] = _rms(x, g_ref[...]).astype(BF16)
    acc_ref[...] = jnp.zeros_like(acc_ref)

    def body(c, carry):
        xn = xn_ref[...]
        g = jnp.dot(xn, wg_ref[c], preferred_element_type=F32)
        u = jnp.dot(xn, wu_ref[c], preferred_element_type=F32)
        h = (g * (1.0 / (1.0 + jnp.exp(-g))) * u).astype(BF16)
        acc_ref[...] += jnp.dot(h, wd_ref[c], preferred_element_type=F32)
        return carry

    lax.fori_loop(0, N_FF_CHUNKS, body, 0)
    y = x + 0.5 * acc_ref[...]
    if final:
        y = _rms(y, gf_ref[...])
    o_ref[...] = y


def _ffn(x, g, wg, wu, wd, gf, layer, *, final, tm):
    n = x.shape[0]
    row = pl.BlockSpec((tm, D_MODEL), lambda i: (i, 0))
    vec = _resident((None, 1, D_MODEL), lambda i: (layer, 0, 0))
    w_up = _resident((None, N_FF_CHUNKS, D_MODEL, FF_CHUNK), lambda i: (layer, 0, 0, 0))
    w_dn = _resident((None, N_FF_CHUNKS, FF_CHUNK, D_MODEL), lambda i: (layer, 0, 0, 0))
    return pl.pallas_call(
        functools.partial(_ffn_kernel, final=final),
        grid=(n // tm,),
        in_specs=[row, vec, w_up, w_up, w_dn, _resident((1, D_MODEL), lambda i: (0, 0))],
        out_specs=row,
        out_shape=jax.ShapeDtypeStruct((n, D_MODEL), F32),
        scratch_shapes=[pltpu.VMEM((tm, D_MODEL), BF16), pltpu.VMEM((tm, D_MODEL), F32)],
        compiler_params=_params(("parallel",)),
        name="ffn",
    )(x, g, wg, wu, wd, gf)


def _rope(v, cos, sin_signed, first_half):
    rot = jnp.where(first_half, pltpu.roll(v, LANES - HEAD_DIM // 2, axis=1),
                    pltpu.roll(v, HEAD_DIM // 2, axis=1))
    return v * cos + rot * sin_signed


def _inproj_kernel(x_ref, g_ref, w_ref, cos_ref, sin_ref,
                   qa_ref, ka_ref, va_ref, qb_ref, kb_ref, vb_ref, xc_ref, gc_ref):
    xn = _rms(x_ref[...], g_ref[...]).astype(BF16)
    cos = cos_ref[...]
    sin = sin_ref[...]
    first_half = lax.broadcasted_iota(jnp.int32, cos.shape, 1) % HEAD_DIM < HEAD_DIM // 2

    def proj(col, width):
        return jnp.dot(xn, w_ref[:, col:col + width], preferred_element_type=F32)

    col = 0
    for s in range(A_SLABS):
        qa_ref[:, s * LANES:(s + 1) * LANES] = _rope(proj(col, LANES), cos, sin, first_half).astype(BF16)
        col += LANES
    ka_ref[...] = _rope(proj(col, LANES), cos, sin, first_half).astype(BF16)
    col += LANES
    va_ref[...] = proj(col, LANES).astype(BF16)
    col += LANES
    for ref, rotary in ((qb_ref, True), (kb_ref, True), (vb_ref, False)):
        for s in range(B_SLABS):
            v = proj(col, LANES)
            ref[s] = _rope(v, cos, sin, first_half) if rotary else v
            col += LANES
    xc_ref[...] = proj(col, C_WIDTH)
    col += C_WIDTH
    gc_ref[...] = proj(col, C_WIDTH)


def _inproj(x, g, w, cos, sin, layer, *, tm):
    n = x.shape[0]
    row = lambda width: pl.BlockSpec((tm, width), lambda i: (i, 0))
    slab = pl.BlockSpec((B_SLABS, tm, LANES), lambda i: (0, i, 0))
    sds = jax.ShapeDtypeStruct
    return pl.pallas_call(
        _inproj_kernel,
        grid=(n // tm,),
        in_specs=[row(D_MODEL),
                  _resident((None, 1, D_MODEL), lambda i: (layer, 0, 0)),
                  _resident((None, D_MODEL, IN_COLS), lambda i: (layer, 0, 0)),
                  row(LANES), row(LANES)],
        out_specs=[row(A_WIDTH), row(A_KV_WIDTH), row(A_KV_WIDTH), slab, slab, slab,
                   row(C_WIDTH), row(C_WIDTH)],
        out_shape=[sds((n, A_WIDTH), BF16), sds((n, A_KV_WIDTH), BF16), sds((n, A_KV_WIDTH), BF16),
                   sds((B_SLABS, n, LANES), F32), sds((B_SLABS, n, LANES), F32),
                   sds((B_SLABS, n, LANES), F32),
                   sds((n, C_WIDTH), F32), sds((n, C_WIDTH), F32)],
        compiler_params=_params(("parallel",)),
        name="inproj",
    )(x, g, w, cos, sin)


def _band_mask(max_dist, prev_valid):
    row = lax.broadcasted_iota(jnp.int32, (2 * BLOCK, 2 * BLOCK), 0) % BLOCK
    col = lax.broadcasted_iota(jnp.int32, (2 * BLOCK, 2 * BLOCK), 1)
    first_col = row + (BLOCK - max_dist)
    if prev_valid is not True:
        first_col = jnp.maximum(first_col, jnp.where(prev_valid, 0, BLOCK))
    return (col >= first_col) & (col <= row + BLOCK)


def _attn_unit(q, kcat, vcat, mask):
    lane = lax.broadcasted_iota(jnp.int32, (BLOCK, LANES), 1)
    left = lane < HEAD_DIM
    zero = jnp.zeros_like(q)
    q2 = jnp.concatenate([jnp.where(left, q, zero), jnp.where(left, zero, q)], axis=0)
    s = lax.dot_general(q2, kcat, (((1,), (1,)), ((), ())), preferred_element_type=F32)
    s = jnp.where(mask, s, -jnp.inf)
    m = jnp.max(s, axis=-1, keepdims=True)
    p = jnp.exp(s - m)
    l = jnp.sum(p, axis=-1, keepdims=True)
    r = jnp.dot(p.astype(BF16), vcat, preferred_element_type=F32) / l
    lse = m + jnp.log(l)
    o = jnp.where(left, r[:BLOCK], r[BLOCK:])
    lse_slab = jnp.where(left, jnp.broadcast_to(lse[:BLOCK], (BLOCK, LANES)),
                         jnp.broadcast_to(lse[BLOCK:], (BLOCK, LANES)))
    return o, lse_slab


def _attn_a_kernel(q_ref, kc_ref, kp_ref, vc_ref, vp_ref, sink_ref, o_ref, *, tq):
    first_tile = pl.program_id(1) == 0
    for j in range(tq // BLOCK):
        rows = slice(j * BLOCK, (j + 1) * BLOCK)
        if j == 0:
            kprev, vprev = kp_ref[...], vp_ref[...]
            mask = _band_mask(A_WINDOW - 1, jnp.logical_not(first_tile))
        else:
            prev_rows = slice((j - 1) * BLOCK, j * BLOCK)
            kprev, vprev = kc_ref[prev_rows, :], vc_ref[prev_rows, :]
            mask = _band_mask(A_WINDOW - 1, True)
        kcat = jnp.concatenate([kprev, kc_ref[rows, :]], axis=0)
        vcat = jnp.concatenate([vprev, vc_ref[rows, :]], axis=0)
        for s in range(A_SLABS):
            lanes = slice(s * LANES, (s + 1) * LANES)
            o, lse = _attn_unit(q_ref[rows, lanes], kcat, vcat, mask)
            o = o * (1.0 / (1.0 + jnp.exp(sink_ref[s:s + 1, :] - lse)))
            o_ref[rows, lanes] = o.astype(BF16)


def _attn_a(qa, ka, va, sinks, *, batch, seq, tq):
    tiles = seq // tq
    ratio = tq // BLOCK
    cur = lambda width: pl.BlockSpec((tq, width), lambda b, i: (b * tiles + i, 0))
    prev = pl.BlockSpec((BLOCK, A_KV_WIDTH),
                        lambda b, i: (jnp.maximum((b * tiles + i) * ratio - 1, 0), 0))
    return pl.pallas_call(
        functools.partial(_attn_a_kernel, tq=tq),
        grid=(batch, tiles),
        in_specs=[cur(A_WIDTH), cur(A_KV_WIDTH), prev, cur(A_KV_WIDTH), prev,
                  _resident((A_SLABS, LANES), lambda b, i: (0, 0))],
        out_specs=cur(A_WIDTH),
        out_shape=jax.ShapeDtypeStruct((batch * seq, A_WIDTH), BF16),
        compiler_params=_params(("parallel", "parallel")),
        name="attn_a",
    )(qa, ka, ka, va, va, sinks)


def _attn_b_kernel(q_ref, kc_ref, kp_ref, vc_ref, vp_ref, o_ref, ob_ref, lb_ref):
    first_tile = pl.program_id(1) == 0
    mask_first = _band_mask(BLOCK, jnp.logical_not(first_tile))
    mask_rest = _band_mask(BLOCK, True)
    for s in range(B_SLABS):
        for br, (window, d) in enumerate(B_BRANCHES):
            assert window // d == BLOCK

            def rows_of(ref, start):
                if d == 1:
                    return ref[s, pl.ds(start, BLOCK), :]
                return ref[s, pl.ds(start, BLOCK, stride=d), :]

            for r in range(d):
                halo = B_TILE - BLOCK * d + r
                kprev = rows_of(kp_ref, halo).astype(BF16)
                vprev = rows_of(vp_ref, halo).astype(BF16)
                for j in range(B_TILE // (d * BLOCK)):
                    start = j * BLOCK * d + r
                    kcur = rows_of(kc_ref, start).astype(BF16)
                    vcur = rows_of(vc_ref, start).astype(BF16)
                    q = rows_of(q_ref, start).astype(BF16)
                    o, lse = _attn_unit(q, jnp.concatenate([kprev, kcur], axis=0),
                                        jnp.concatenate([vprev, vcur], axis=0),
                                        mask_first if j == 0 else mask_rest)
                    dst = pl.ds(start, BLOCK) if d == 1 else pl.ds(start, BLOCK, stride=d)
                    ob_ref[br, dst, :] = o
                    lb_ref[br, dst, :] = lse
                    kprev, vprev = kcur, vcur
        lses = [lb_ref[br] for br in range(len(B_BRANCHES))]
        m = functools.reduce(jnp.maximum, lses)
        es = [jnp.exp(x - m) for x in lses]
        tot = functools.reduce(jnp.add, es)
        acc = ob_ref[0] * (es[0] / tot)
        for br in range(1, len(B_BRANCHES)):
            acc = acc + ob_ref[br] * (es[br] / tot)
        o_ref[:, s * LANES:(s + 1) * LANES] = acc.astype(BF16)


def _attn_b(qb, kb, vb, *, batch, seq):
    tiles = seq // B_TILE
    cur = pl.BlockSpec((B_SLABS, B_TILE, LANES), lambda b, i: (0, b * tiles + i, 0))
    prev = pl.BlockSpec((B_SLABS, B_TILE, LANES),
                        lambda b, i: (0, jnp.maximum(b * tiles + i - 1, 0), 0))
    nbr = len(B_BRANCHES)
    return pl.pallas_call(
        _attn_b_kernel,
        grid=(batch, tiles),
        in_specs=[cur, cur, prev, cur, prev],
        out_specs=pl.BlockSpec((B_TILE, B_WIDTH), lambda b, i: (b * tiles + i, 0)),
        out_shape=jax.ShapeDtypeStruct((batch * seq, B_WIDTH), BF16),
        scratch_shapes=[pltpu.VMEM((nbr, B_TILE, LANES), F32),
                        pltpu.VMEM((nbr, B_TILE, LANES), F32)],
        compiler_params=_params(("parallel", "parallel")),
        name="attn_b",
    )(qb, kb, kb, vb, vb)


def _rglru_kernel(xc_ref, xh_ref, gc_ref, keep_ref, cw_ref, cb_ref, wg_ref, bg_ref, lam_ref,
                  o_ref, xbuf_ref, a_ref, b_ref, carry_ref, *, tm):
    first_tile = pl.program_id(1) == 0

    @pl.when(first_tile)
    def _():
        carry_ref[...] = jnp.zeros_like(carry_ref)

    xbuf_ref[0:SUBLANES, :] = jnp.where(first_tile, 0.0, xh_ref[...])
    xbuf_ref[SUBLANES:, :] = xc_ref[...]
    y = cb_ref[...] + cw_ref[0:1, :] * xc_ref[...]
    for j in range(1, C_CONV):
        y = y + cw_ref[j:j + 1, :] * xbuf_ref[pl.ds(SUBLANES - j, tm), :]

    gates = jnp.dot(y.astype(BF16), wg_ref[...], preferred_element_type=F32) + bg_ref[...]
    gates = 1.0 / (1.0 + jnp.exp(-gates))
    r, ig = gates[:, :C_WIDTH], gates[:, C_WIDTH:]
    lam = lam_ref[...]
    softplus = jnp.maximum(-lam, 0.0) + jnp.log(1.0 + jnp.exp(-jnp.abs(lam)))
    log_a = -C_EXP * r * softplus
    keep = jnp.concatenate([keep_ref[...]] * (C_WIDTH // LANES), axis=1) != 0.0
    a = jnp.where(keep, jnp.exp(log_a), 0.0)
    mult = jnp.where(keep, jnp.sqrt(1.0 - jnp.exp(2.0 * log_a)), 1.0)
    b = mult * (ig * y)

    in_group = lax.broadcasted_iota(jnp.int32, a.shape, 0) % SUBLANES
    sh = 1
    while sh < SUBLANES:
        ok = in_group >= sh
        b = b + a * jnp.where(ok, pltpu.roll(b, sh, axis=0), 0.0)
        a = a * jnp.where(ok, pltpu.roll(a, sh, axis=0), 1.0)
        sh *= 2
    a_ref[...] = a
    b_ref[...] = b

    def group(g, carry):
        rows = pl.ds(pl.multiple_of(g * SUBLANES, SUBLANES), SUBLANES)
        h = b_ref[rows, :] + a_ref[rows, :] * carry
        b_ref[rows, :] = h
        return h[SUBLANES - 1:SUBLANES, :]

    carry_ref[...] = lax.fori_loop(0, tm // SUBLANES, group, carry_ref[...])

    gate = gc_ref[...]
    gelu = 0.5 * gate * (1.0 + jnp.tanh(0.7978845608028654 * (gate + 0.044715 * gate * gate * gate)))
    o_ref[...] = (b_ref[...] * gelu).astype(BF16)


def _rglru(xc, gc, keep, cw, cb, wg, bg, lam, layer, *, batch, seq, tm):
    tiles = seq // tm
    ratio = tm // SUBLANES
    row = lambda width: pl.BlockSpec((tm, width), lambda b, i: (b * tiles + i, 0))
    halo = pl.BlockSpec((SUBLANES, C_WIDTH),
                        lambda b, i: (jnp.maximum((b * tiles + i) * ratio - 1, 0), 0))
    per_layer = lambda rows, cols: _resident((None, rows, cols), lambda b, i: (layer, 0, 0))
    return pl.pallas_call(
        functools.partial(_rglru_kernel, tm=tm),
        grid=(batch, tiles),
        in_specs=[row(C_WIDTH), halo, row(C_WIDTH), row(LANES),
                  per_layer(C_CONV, C_WIDTH), per_layer(1, C_WIDTH),
                  per_layer(C_WIDTH, 2 * C_WIDTH), per_layer(1, 2 * C_WIDTH), per_layer(1, C_WIDTH)],
        out_specs=row(C_WIDTH),
        out_shape=jax.ShapeDtypeStruct((batch * seq, C_WIDTH), BF16),
        scratch_shapes=[pltpu.VMEM((tm + SUBLANES, C_WIDTH), F32),
                        pltpu.VMEM((tm, C_WIDTH), F32), pltpu.VMEM((tm, C_WIDTH), F32),
                        pltpu.VMEM((1, C_WIDTH), F32)],
        compiler_params=_params(("arbitrary", "arbitrary")),
        name="rglru",
    )(xc, xc, gc, keep, cw, cb, wg, bg, lam)


def _outproj_kernel(x_ref, a_ref, b_ref, c_ref, w_ref, o_ref):
    mix = jnp.concatenate([a_ref[...], b_ref[...], c_ref[...]], axis=1)
    o_ref[...] = x_ref[...] + jnp.dot(mix, w_ref[...], preferred_element_type=F32)


def _outproj(x, oa, ob, oc, w, layer, *, tm):
    n = x.shape[0]
    row = lambda width: pl.BlockSpec((tm, width), lambda i: (i, 0))
    return pl.pallas_call(
        _outproj_kernel,
        grid=(n // tm,),
        in_specs=[row(D_MODEL), row(A_WIDTH), row(B_WIDTH), row(C_WIDTH),
                  _resident((None, MIX_WIDTH, D_MODEL), lambda i: (layer, 0, 0))],
        out_specs=row(D_MODEL),
        out_shape=jax.ShapeDtypeStruct((n, D_MODEL), F32),
        compiler_params=_params(("parallel",)),
        name="outproj",
    )(x, oa, ob, oc, w)


def _a_head_order():
    group = A_Q_HEADS // A_KV_HEADS
    return [kv * group + m for m in range(group) for kv in range(A_KV_HEADS)]


def _a_column_perm():
    cols = []
    for h in _a_head_order():
        cols.extend(range(h * HEAD_DIM, (h + 1) * HEAD_DIM))
    return jnp.array(cols, dtype=jnp.int32)


def _chunk_cols(w):
    depth = w.shape[0]
    return w.astype(BF16).reshape(depth, D_MODEL, N_FF_CHUNKS, FF_CHUNK).transpose(0, 2, 1, 3)


def _chunk_rows(w):
    return w.astype(BF16).reshape(w.shape[0], N_FF_CHUNKS, FF_CHUNK, D_MODEL)


def _block_diag(w):
    depth, blocks, n, _ = w.shape
    eye = jnp.eye(blocks, dtype=w.dtype)
    return jnp.einsum('lhij,hg->lhigj', w, eye).reshape(depth, blocks * n, blocks * n)


def kernel(x, positions, norm_ffn1, ffn1_gate, ffn1_up, ffn1_down, norm_mix, w_in, attn_sinks,
           conv_w, conv_b, rg_w_r, rg_b_r, rg_w_i, rg_b_i, rg_lambda, w_out,
           norm_ffn2, ffn2_gate, ffn2_up, ffn2_down, norm_final):
    batch, seq, _ = x.shape
    depth = w_in.shape[0]
    n = batch * seq
    assert seq % B_TILE == 0
    tm = 512

    perm = _a_column_perm()
    q_cols = jnp.concatenate([w_in[:, :, :A_WIDTH][:, :, perm] * SCALE,
                              w_in[:, :, A_WIDTH:A_WIDTH + 2 * A_KV_WIDTH],
                              w_in[:, :, A_WIDTH + 2 * A_KV_WIDTH:A_WIDTH + 2 * A_KV_WIDTH + B_WIDTH] * SCALE,
                              w_in[:, :, A_WIDTH + 2 * A_KV_WIDTH + B_WIDTH:]], axis=2)
    w_in_b = q_cols.astype(BF16)
    w_out_b = jnp.concatenate([w_out[:, :A_WIDTH][:, perm], w_out[:, A_WIDTH:]], axis=1).astype(BF16)
    sinks = jnp.repeat(attn_sinks[:, jnp.array(_a_head_order())], HEAD_DIM, axis=1)
    sinks = sinks.reshape(depth, A_SLABS, LANES)
    w_gates = jnp.concatenate([_block_diag(rg_w_r), _block_diag(rg_w_i)], axis=2).astype(BF16)
    b_gates = jnp.concatenate([rg_b_r.reshape(depth, 1, C_WIDTH), rg_b_i.reshape(depth, 1, C_WIDTH)], axis=2)
    ffn_w = [(_chunk_cols(g), _chunk_cols(u), _chunk_rows(d))
             for g, u, d in ((ffn1_gate, ffn1_up, ffn1_down), (ffn2_gate, ffn2_up, ffn2_down))]
    vec = lambda v: v.reshape(depth, 1, -1)
    gf = norm_final.reshape(1, D_MODEL)

    cos, sin, keep = _rope_tables(positions)
    h = x.reshape(n, D_MODEL)
    for l in range(depth):
        h = _ffn(h, vec(norm_ffn1), *ffn_w[0], gf, l, final=False, tm=tm)
        qa, ka, va, qb, kb, vb, xc, gc = _inproj(h, vec(norm_mix), w_in_b, cos, sin, l, tm=tm)
        oa = _attn_a(qa, ka, va, sinks[l], batch=batch, seq=seq, tq=tm)
        ob = _attn_b(qb, kb, vb, batch=batch, seq=seq)
        oc = _rglru(xc, gc, keep, conv_w, vec(conv_b), w_gates, b_gates, vec(rg_lambda), l,
                    batch=batch, seq=seq, tm=tm)
        h = _outproj(h, oa, ob, oc, w_out_b, l, tm=tm)
        h = _ffn(h, vec(norm_ffn2), *ffn_w[1], gf, l, final=(l == depth - 1), tm=tm)
    return h.reshape(batch, seq, D_MODEL)
```

```python
import functools

import jax
import jax.numpy as jnp
from jax import lax
from jax.experimental import pallas as pl
from jax.experimental.pallas import tpu as pltpu

D_MODEL = 1024
HEAD_DIM = 64
A_Q_HEADS = 4
A_KV_HEADS = 2
A_WINDOW = 128
B_HEADS = 6
B_BRANCHES = ((128, 1), (512, 4), (2048, 16))
C_WIDTH = 384
C_BLOCKS = 6
C_CONV = 4
C_EXP = 8.0
D_FF = 2816
BLOCK = 128
ROPE_THETA = 10000.0
EPS = 1e-6
SCALE = HEAD_DIM ** -0.5

A_WIDTH = A_Q_HEADS * HEAD_DIM
A_KV_WIDTH = A_KV_HEADS * HEAD_DIM
B_WIDTH = B_HEADS * HEAD_DIM
MIX_WIDTH = A_WIDTH + B_WIDTH + C_WIDTH
IN_COLS = A_WIDTH + 2 * A_KV_WIDTH + 3 * B_WIDTH + 2 * C_WIDTH

LANES = 128
SUBLANES = 8
FF_CHUNK = 256
N_FF_CHUNKS = D_FF // FF_CHUNK
B_SLABS = B_WIDTH // LANES
A_SLABS = A_WIDTH // LANES
MAX_DIL = max(d for _, d in B_BRANCHES)
B_TILE = BLOCK * MAX_DIL
VMEM_LIMIT = 56 * 1024 * 1024

F32 = jnp.float32
BF16 = jnp.bfloat16


def _params(sem):
    return pltpu.CompilerParams(dimension_semantics=sem, vmem_limit_bytes=VMEM_LIMIT)


def _resident(shape, index_map):
    return pl.BlockSpec(shape, index_map, pipeline_mode=pl.Buffered(1))


def _rms(x, g):
    return x * lax.rsqrt(jnp.mean(x * x, axis=-1, keepdims=True) + EPS) * g


def _tables_kernel(pos_ref, inv_ref, sign_ref, cos_ref, sin_ref, keep_ref):
    inv = inv_ref[...]
    sign = sign_ref[...]
    for r in range(pos_ref.shape[0]):
        p = pos_ref[r:r + 1, :].astype(F32)
        ang_t = inv * p
        rows = pl.ds(r * LANES, LANES)
        cos_ref[rows, :] = jnp.cos(ang_t).T
        sin_ref[rows, :] = (jnp.sin(ang_t) * sign).T
        keep_ref[rows, :] = jnp.where(jnp.broadcast_to(p, (LANES, LANES)) == 0.0, 0.0, 1.0).T


def _rope_tables(positions):
    n = positions.size
    rows = SUBLANES
    pos2 = positions.reshape(n // LANES, LANES)
    half = HEAD_DIM // 2
    inv = 1.0 / (ROPE_THETA ** (jnp.arange(0, HEAD_DIM, 2, dtype=F32) / HEAD_DIM))
    inv_col = jnp.tile(inv, LANES // half).reshape(LANES, 1)
    sign_col = jnp.tile(jnp.concatenate([-jnp.ones(half, F32), jnp.ones(half, F32)]),
                        LANES // HEAD_DIM).reshape(LANES, 1)
    tab = jax.ShapeDtypeStruct((n, LANES), F32)
    out_spec = pl.BlockSpec((rows * LANES, LANES), lambda i: (i, 0))
    return pl.pallas_call(
        _tables_kernel,
        grid=(n // (rows * LANES),),
        in_specs=[pl.BlockSpec((rows, LANES), lambda i: (i, 0)),
                  pl.BlockSpec((LANES, 1), lambda i: (0, 0)),
                  pl.BlockSpec((LANES, 1), lambda i: (0, 0))],
        out_specs=[out_spec, out_spec, out_spec],
        out_shape=[tab, tab, tab],
        compiler_params=_params(("arbitrary",)),
        name="rope_tables",
    )(pos2, inv_col, sign_col)


def _ffn_kernel(*refs, mixed, final):
    if mixed:
        x_in_ref, a_ref, b_ref, c_ref, wo_ref, *refs, x_ref = refs
        mix = jnp.concatenate([a_ref[...], b_ref[...], c_ref[...]], axis=1)
        x_ref[...] = x_in_ref[...] + jnp.dot(mix, wo_ref[...], preferred_element_type=F32)
    else:
        x_ref, *refs = refs
    g_ref, wg_ref, wu_ref, wd_ref, gf_ref, o_ref, xn_ref, acc_ref = refs
    xn_ref[...] = _rms(x_ref[...], g_ref[...]).astype(BF16)
    for c in range(N_FF_CHUNKS):
        xn = xn_ref[...]
        cols = slice(c * FF_CHUNK, (c + 1) * FF_CHUNK)
        g = jnp.dot(xn, wg_ref[:, cols], preferred_element_type=F32)
        u = jnp.dot(xn, wu_ref[:, cols], preferred_element_type=F32)
        h = (g * (1.0 / (1.0 + jnp.exp(-g))) * u).astype(BF16)
        d = jnp.dot(h, wd_ref[cols, :], preferred_element_type=F32)
        if c == 0:
            acc_ref[...] = d
        elif c < N_FF_CHUNKS - 1:
            acc_ref[...] += d
        else:
            y = x_ref[...] + 0.5 * (acc_ref[...] + d)
            if final:
                y = _rms(y, gf_ref[...])
            o_ref[...] = y


def _ffn(x, g, wg, wu, wd, gf, layer, *, final, tm, mix=None):
    n = x.shape[0]
    row = lambda width: pl.BlockSpec((tm, width), lambda i: (i, 0))
    per_layer = lambda rows, cols: _resident((None, rows, cols), lambda i: (layer, 0, 0))
    in_specs = [row(D_MODEL)]
    args = [x]
    scratch = [pltpu.VMEM((tm, D_MODEL), BF16), pltpu.VMEM((tm, D_MODEL), F32)]
    if mix is not None:
        in_specs += [row(A_WIDTH), row(B_WIDTH), row(C_WIDTH), per_layer(MIX_WIDTH, D_MODEL)]
        args += list(mix)
        scratch.append(pltpu.VMEM((tm, D_MODEL), F32))
    in_specs += [per_layer(1, D_MODEL), per_layer(D_MODEL, D_FF), per_layer(D_MODEL, D_FF),
                 per_layer(D_FF, D_MODEL), _resident((1, D_MODEL), lambda i: (0, 0))]
    args += [g, wg, wu, wd, gf]
    return pl.pallas_call(
        functools.partial(_ffn_kernel, mixed=mix is not None, final=final),
        grid=(n // tm,),
        in_specs=in_specs,
        out_specs=row(D_MODEL),
        out_shape=jax.ShapeDtypeStruct((n, D_MODEL), F32),
        scratch_shapes=scratch,
        compiler_params=_params(("parallel",)),
        name="ffn_mix" if mix is not None else "ffn",
    )(*args)


def _rope(v, cos, sin_signed, first_half):
    rot = jnp.where(first_half, pltpu.roll(v, LANES - HEAD_DIM // 2, axis=1),
                    pltpu.roll(v, HEAD_DIM // 2, axis=1))
    return v * cos + rot * sin_signed


def _inproj_kernel(x_ref, g_ref, w_ref, cos_ref, sin_ref,
                   qa_ref, ka_ref, va_ref, qb_ref, kb_ref, vb_ref, xc_ref, gc_ref):
    xn = _rms(x_ref[...], g_ref[...]).astype(BF16)
    cos = cos_ref[...]
    sin = sin_ref[...]
    first_half = lax.broadcasted_iota(jnp.int32, cos.shape, 1) % HEAD_DIM < HEAD_DIM // 2

    group = 4 * LANES
    groups = [jnp.dot(xn, w_ref[:, c:min(c + group, IN_COLS)], preferred_element_type=F32)
              for c in range(0, IN_COLS, group)]

    def proj(col, width):
        parts = [groups[c // group][:, c % group:c % group + LANES] for c in range(col, col + width, LANES)]
        return parts[0] if len(parts) == 1 else jnp.concatenate(parts, axis=1)

    col = 0
    for s in range(A_SLABS):
        qa_ref[:, s * LANES:(s + 1) * LANES] = _rope(proj(col, LANES), cos, sin, first_half).astype(BF16)
        col += LANES
    ka_ref[...] = _rope(proj(col, LANES), cos, sin, first_half).astype(BF16)
    col += LANES
    va_ref[...] = proj(col, LANES).astype(BF16)
    col += LANES
    for ref, rotary in ((qb_ref, True), (kb_ref, True), (vb_ref, False)):
        for s in range(B_SLABS):
            v = proj(col, LANES)
            ref[s] = _rope(v, cos, sin, first_half) if rotary else v
            col += LANES
    xc_ref[...] = proj(col, C_WIDTH)
    col += C_WIDTH
    gc_ref[...] = proj(col, C_WIDTH)


def _inproj(x, g, w, cos, sin, layer, *, tm):
    n = x.shape[0]
    row = lambda width: pl.BlockSpec((tm, width), lambda i: (i, 0))
    slab = pl.BlockSpec((B_SLABS, tm, LANES), lambda i: (0, i, 0))
    sds = jax.ShapeDtypeStruct
    return pl.pallas_call(
        _inproj_kernel,
        grid=(n // tm,),
        in_specs=[row(D_MODEL),
                  _resident((None, 1, D_MODEL), lambda i: (layer, 0, 0)),
                  _resident((None, D_MODEL, IN_COLS), lambda i: (layer, 0, 0)),
                  row(LANES), row(LANES)],
        out_specs=[row(A_WIDTH), row(A_KV_WIDTH), row(A_KV_WIDTH), slab, slab, slab,
                   row(C_WIDTH), row(C_WIDTH)],
        out_shape=[sds((n, A_WIDTH), BF16), sds((n, A_KV_WIDTH), BF16), sds((n, A_KV_WIDTH), BF16),
                   sds((B_SLABS, n, LANES), F32), sds((B_SLABS, n, LANES), F32),
                   sds((B_SLABS, n, LANES), F32),
                   sds((n, C_WIDTH), F32), sds((n, C_WIDTH), F32)],
        compiler_params=_params(("parallel",)),
        name="inproj",
    )(x, g, w, cos, sin)


def _band_mask(max_dist, prev_valid):
    row = lax.broadcasted_iota(jnp.int32, (2 * BLOCK, 2 * BLOCK), 0) % BLOCK
    col = lax.broadcasted_iota(jnp.int32, (2 * BLOCK, 2 * BLOCK), 1)
    first_col = row + (BLOCK - max_dist)
    if prev_valid is not True:
        first_col = jnp.maximum(first_col, jnp.where(prev_valid, 0, BLOCK))
    return (col >= first_col) & (col <= row + BLOCK)


def _attn_unit(q, kcat, vcat, mask):
    lane = lax.broadcasted_iota(jnp.int32, (BLOCK, LANES), 1)
    left = lane < HEAD_DIM
    zero = jnp.zeros_like(q)
    q2 = jnp.concatenate([jnp.where(left, q, zero), jnp.where(left, zero, q)], axis=0)
    s = lax.dot_general(q2, kcat, (((1,), (1,)), ((), ())), preferred_element_type=F32)
    s = jnp.where(mask, s, -jnp.inf)
    m = jnp.max(s, axis=-1, keepdims=True)
    p = jnp.exp(s - m)
    l = jnp.sum(p, axis=-1, keepdims=True)
    r = jnp.dot(p.astype(BF16), vcat, preferred_element_type=F32) / l
    lse = m + jnp.log(l)
    o = jnp.where(left, r[:BLOCK], r[BLOCK:])
    lse_slab = jnp.where(left, jnp.broadcast_to(lse[:BLOCK], (BLOCK, LANES)),
                         jnp.broadcast_to(lse[BLOCK:], (BLOCK, LANES)))
    return o, lse_slab


def _attn_a_kernel(q_ref, kc_ref, kp_ref, vc_ref, vp_ref, sink_ref, o_ref, *, tq):
    first_tile = pl.program_id(1) == 0
    for j in range(tq // BLOCK):
        rows = slice(j * BLOCK, (j + 1) * BLOCK)
        if j == 0:
            kprev, vprev = kp_ref[...], vp_ref[...]
            mask = _band_mask(A_WINDOW - 1, jnp.logical_not(first_tile))
        else:
            prev_rows = slice((j - 1) * BLOCK, j * BLOCK)
            kprev, vprev = kc_ref[prev_rows, :], vc_ref[prev_rows, :]
            mask = _band_mask(A_WINDOW - 1, True)
        kcat = jnp.concatenate([kprev, kc_ref[rows, :]], axis=0)
        vcat = jnp.concatenate([vprev, vc_ref[rows, :]], axis=0)
        for s in range(A_SLABS):
            lanes = slice(s * LANES, (s + 1) * LANES)
            o, lse = _attn_unit(q_ref[rows, lanes], kcat, vcat, mask)
            o = o * (1.0 / (1.0 + jnp.exp(sink_ref[s:s + 1, :] - lse)))
            o_ref[rows, lanes] = o.astype(BF16)


def _attn_a(qa, ka, va, sinks, *, batch, seq, tq):
    tiles = seq // tq
    ratio = tq // BLOCK
    cur = lambda width: pl.BlockSpec((tq, width), lambda b, i: (b * tiles + i, 0))
    prev = pl.BlockSpec((BLOCK, A_KV_WIDTH),
                        lambda b, i: (jnp.maximum((b * tiles + i) * ratio - 1, 0), 0))
    return pl.pallas_call(
        functools.partial(_attn_a_kernel, tq=tq),
        grid=(batch, tiles),
        in_specs=[cur(A_WIDTH), cur(A_KV_WIDTH), prev, cur(A_KV_WIDTH), prev,
                  _resident((A_SLABS, LANES), lambda b, i: (0, 0))],
        out_specs=cur(A_WIDTH),
        out_shape=jax.ShapeDtypeStruct((batch * seq, A_WIDTH), BF16),
        compiler_params=_params(("parallel", "parallel")),
        name="attn_a",
    )(qa, ka, ka, va, va, sinks)


def _attn_b_kernel(q_ref, kc_ref, kp_ref, vc_ref, vp_ref, o_ref, ob_ref, lb_ref):
    first_tile = pl.program_id(1) == 0
    mask_first = _band_mask(BLOCK, jnp.logical_not(first_tile))
    mask_rest = _band_mask(BLOCK, True)
    for s in range(B_SLABS):
        for br, (window, d) in enumerate(B_BRANCHES):
            assert window // d == BLOCK

            def rows_of(ref, start):
                if d == 1:
                    return ref[s, pl.ds(start, BLOCK), :]
                return ref[s, pl.ds(start, BLOCK, stride=d), :]

            for r in range(d):
                halo = B_TILE - BLOCK * d + r
                kprev = rows_of(kp_ref, halo).astype(BF16)
                vprev = rows_of(vp_ref, halo).astype(BF16)
                for j in range(B_TILE // (d * BLOCK)):
                    start = j * BLOCK * d + r
                    kcur = rows_of(kc_ref, start).astype(BF16)
                    vcur = rows_of(vc_ref, start).astype(BF16)
                    q = rows_of(q_ref, start).astype(BF16)
                    o, lse = _attn_unit(q, jnp.concatenate([kprev, kcur], axis=0),
                                        jnp.concatenate([vprev, vcur], axis=0),
                                        mask_first if j == 0 else mask_rest)
                    dst = pl.ds(start, BLOCK) if d == 1 else pl.ds(start, BLOCK, stride=d)
                    ob_ref[br, dst, :] = o
                    lb_ref[br, dst, :] = lse
                    kprev, vprev = kcur, vcur
        lses = [lb_ref[br] for br in range(len(B_BRANCHES))]
        m = functools.reduce(jnp.maximum, lses)
        es = [jnp.exp(x - m) for x in lses]
        tot = functools.reduce(jnp.add, es)
        acc = ob_ref[0] * (es[0] / tot)
        for br in range(1, len(B_BRANCHES)):
            acc = acc + ob_ref[br] * (es[br] / tot)
        o_ref[:, s * LANES:(s + 1) * LANES] = acc.astype(BF16)


def _attn_b(qb, kb, vb, *, batch, seq):
    tiles = seq // B_TILE
    cur = pl.BlockSpec((B_SLABS, B_TILE, LANES), lambda b, i: (0, b * tiles + i, 0))
    prev = pl.BlockSpec((B_SLABS, B_TILE, LANES),
                        lambda b, i: (0, jnp.maximum(b * tiles + i - 1, 0), 0))
    nbr = len(B_BRANCHES)
    return pl.pallas_call(
        _attn_b_kernel,
        grid=(batch, tiles),
        in_specs=[cur, cur, prev, cur, prev],
        out_specs=pl.BlockSpec((B_TILE, B_WIDTH), lambda b, i: (b * tiles + i, 0)),
        out_shape=jax.ShapeDtypeStruct((batch * seq, B_WIDTH), BF16),
        scratch_shapes=[pltpu.VMEM((nbr, B_TILE, LANES), F32),
                        pltpu.VMEM((nbr, B_TILE, LANES), F32)],
        compiler_params=_params(("parallel", "parallel")),
        name="attn_b",
    )(qb, kb, kb, vb, vb)


def _rglru_kernel(xc_ref, xh_ref, gc_ref, keep_ref, cw_ref, cb_ref, wg_ref, bg_ref, lam_ref,
                  o_ref, xbuf_ref, a_ref, b_ref, carry_ref, *, tm):
    first_tile = pl.program_id(1) == 0

    @pl.when(first_tile)
    def _():
        carry_ref[...] = jnp.zeros_like(carry_ref)

    xbuf_ref[0:SUBLANES, :] = jnp.where(first_tile, 0.0, xh_ref[...])
    xbuf_ref[SUBLANES:, :] = xc_ref[...]
    y = cb_ref[...] + cw_ref[0:1, :] * xc_ref[...]
    for j in range(1, C_CONV):
        y = y + cw_ref[j:j + 1, :] * xbuf_ref[pl.ds(SUBLANES - j, tm), :]

    gates = jnp.dot(y.astype(BF16), wg_ref[...], preferred_element_type=F32) + bg_ref[...]
    gates = 1.0 / (1.0 + jnp.exp(-gates))
    r, ig = gates[:, :C_WIDTH], gates[:, C_WIDTH:]
    lam = lam_ref[...]
    softplus = jnp.maximum(-lam, 0.0) + jnp.log(1.0 + jnp.exp(-jnp.abs(lam)))
    log_a = -C_EXP * r * softplus
    keep = jnp.concatenate([keep_ref[...]] * (C_WIDTH // LANES), axis=1) != 0.0
    a = jnp.where(keep, jnp.exp(log_a), 0.0)
    mult = jnp.where(keep, jnp.sqrt(1.0 - jnp.exp(2.0 * log_a)), 1.0)
    b = mult * (ig * y)

    in_group = lax.broadcasted_iota(jnp.int32, a.shape, 0) % SUBLANES
    sh = 1
    while sh < SUBLANES:
        ok = in_group >= sh
        b = b + a * jnp.where(ok, pltpu.roll(b, sh, axis=0), 0.0)
        a = a * jnp.where(ok, pltpu.roll(a, sh, axis=0), 1.0)
        sh *= 2
    a_ref[...] = a
    b_ref[...] = b

    def group(g, carry):
        rows = pl.ds(pl.multiple_of(g * SUBLANES, SUBLANES), SUBLANES)
        h = b_ref[rows, :] + a_ref[rows, :] * carry
        b_ref[rows, :] = h
        return h[SUBLANES - 1:SUBLANES, :]

    carry_ref[...] = lax.fori_loop(0, tm // SUBLANES, group, carry_ref[...])

    gate = gc_ref[...]
    gelu = 0.5 * gate * (1.0 + jnp.tanh(0.7978845608028654 * (gate + 0.044715 * gate * gate * gate)))
    o_ref[...] = (b_ref[...] * gelu).astype(BF16)


def _rglru(xc, gc, keep, cw, cb, wg, bg, lam, layer, *, batch, seq, tm):
    tiles = seq // tm
    ratio = tm // SUBLANES
    row = lambda width: pl.BlockSpec((tm, width), lambda b, i: (b * tiles + i, 0))
    halo = pl.BlockSpec((SUBLANES, C_WIDTH),
                        lambda b, i: (jnp.maximum((b * tiles + i) * ratio - 1, 0), 0))
    per_layer = lambda rows, cols: _resident((None, rows, cols), lambda b, i: (layer, 0, 0))
    return pl.pallas_call(
        functools.partial(_rglru_kernel, tm=tm),
        grid=(batch, tiles),
        in_specs=[row(C_WIDTH), halo, row(C_WIDTH), row(LANES),
                  per_layer(C_CONV, C_WIDTH), per_layer(1, C_WIDTH),
                  per_layer(C_WIDTH, 2 * C_WIDTH), per_layer(1, 2 * C_WIDTH), per_layer(1, C_WIDTH)],
        out_specs=row(C_WIDTH),
        out_shape=jax.ShapeDtypeStruct((batch * seq, C_WIDTH), BF16),
        scratch_shapes=[pltpu.VMEM((tm + SUBLANES, C_WIDTH), F32),
                        pltpu.VMEM((tm, C_WIDTH), F32), pltpu.VMEM((tm, C_WIDTH), F32),
                        pltpu.VMEM((1, C_WIDTH), F32)],
        compiler_params=_params(("arbitrary", "arbitrary")),
        name="rglru",
    )(xc, xc, gc, keep, cw, cb, wg, bg, lam)


def _a_head_order():
    group = A_Q_HEADS // A_KV_HEADS
    return [kv * group + m for m in range(group) for kv in range(A_KV_HEADS)]


def _a_column_perm():
    cols = []
    for h in _a_head_order():
        cols.extend(range(h * HEAD_DIM, (h + 1) * HEAD_DIM))
    return jnp.array(cols, dtype=jnp.int32)


def _block_diag(w):
    depth, blocks, n, _ = w.shape
    eye = jnp.eye(blocks, dtype=w.dtype)
    return jnp.einsum('lhij,hg->lhigj', w, eye).reshape(depth, blocks * n, blocks * n)


def kernel(x, positions, norm_ffn1, ffn1_gate, ffn1_up, ffn1_down, norm_mix, w_in, attn_sinks,
           conv_w, conv_b, rg_w_r, rg_b_r, rg_w_i, rg_b_i, rg_lambda, w_out,
           norm_ffn2, ffn2_gate, ffn2_up, ffn2_down, norm_final):
    batch, seq, _ = x.shape
    depth = w_in.shape[0]
    n = batch * seq
    assert seq % B_TILE == 0
    tm = 512

    perm = _a_column_perm()
    q_cols = jnp.concatenate([w_in[:, :, :A_WIDTH][:, :, perm] * SCALE,
                              w_in[:, :, A_WIDTH:A_WIDTH + 2 * A_KV_WIDTH],
                              w_in[:, :, A_WIDTH + 2 * A_KV_WIDTH:A_WIDTH + 2 * A_KV_WIDTH + B_WIDTH] * SCALE,
                              w_in[:, :, A_WIDTH + 2 * A_KV_WIDTH + B_WIDTH:]], axis=2)
    w_in_b = q_cols.astype(BF16)
    w_out_b = jnp.concatenate([w_out[:, :A_WIDTH][:, perm], w_out[:, A_WIDTH:]], axis=1).astype(BF16)
    sinks = jnp.repeat(attn_sinks[:, jnp.array(_a_head_order())], HEAD_DIM, axis=1)
    sinks = sinks.reshape(depth, A_SLABS, LANES)
    w_gates = jnp.concatenate([_block_diag(rg_w_r), _block_diag(rg_w_i)], axis=2).astype(BF16)
    b_gates = jnp.concatenate([rg_b_r.reshape(depth, 1, C_WIDTH), rg_b_i.reshape(depth, 1, C_WIDTH)], axis=2)
    ffn_w = [(g.astype(BF16), u.astype(BF16), d.astype(BF16))
             for g, u, d in ((ffn1_gate, ffn1_up, ffn1_down), (ffn2_gate, ffn2_up, ffn2_down))]
    vec = lambda v: v.reshape(depth, 1, -1)
    gf = norm_final.reshape(1, D_MODEL)

    cos, sin, keep = _rope_tables(positions)
    h = x.reshape(n, D_MODEL)
    for l in range(depth):
        h = _ffn(h, vec(norm_ffn1), *ffn_w[0], gf, l, final=False, tm=tm)
        qa, ka, va, qb, kb, vb, xc, gc = _inproj(h, vec(norm_mix), w_in_b, cos, sin, l, tm=tm)
        oa = _attn_a(qa, ka, va, sinks[l], batch=batch, seq=seq, tq=tm)
        ob = _attn_b(qb, kb, vb, batch=batch, seq=seq)
        oc = _rglru(xc, gc, keep, conv_w, vec(conv_b), w_gates, b_gates, vec(rg_lambda), l,
                    batch=batch, seq=seq, tm=tm)
        h = _ffn(h, vec(norm_ffn2), *ffn_w[1], gf, l, final=(l == depth - 1), tm=tm,
                 mix=(oa, ob, oc, w_out_b))
    return h.reshape(batch, seq, D_MODEL)
```

```python
import functools

import jax
import jax.numpy as jnp
from jax import lax
from jax.experimental import pallas as pl
from jax.experimental.pallas import tpu as pltpu

D_MODEL = 1024
HEAD_DIM = 64
A_Q_HEADS = 4
A_KV_HEADS = 2
A_WINDOW = 128
B_HEADS = 6
B_BRANCHES = ((128, 1), (512, 4), (2048, 16))
C_WIDTH = 384
C_BLOCKS = 6
C_CONV = 4
C_EXP = 8.0
D_FF = 2816
BLOCK = 128
ROPE_THETA = 10000.0
EPS = 1e-6
SCALE = HEAD_DIM ** -0.5

A_WIDTH = A_Q_HEADS * HEAD_DIM
A_KV_WIDTH = A_KV_HEADS * HEAD_DIM
B_WIDTH = B_HEADS * HEAD_DIM
MIX_WIDTH = A_WIDTH + B_WIDTH + C_WIDTH
IN_COLS = A_WIDTH + 2 * A_KV_WIDTH + 3 * B_WIDTH + 2 * C_WIDTH

LANES = 128
SUBLANES = 8
FF_CHUNK = 256
N_FF_CHUNKS = D_FF // FF_CHUNK
B_SLABS = B_WIDTH // LANES
A_SLABS = A_WIDTH // LANES
MAX_DIL = max(d for _, d in B_BRANCHES)
B_TILE = BLOCK * MAX_DIL
B_SPLIT = 4
VMEM_LIMIT = 56 * 1024 * 1024

F32 = jnp.float32
BF16 = jnp.bfloat16


def _params(sem):
    return pltpu.CompilerParams(dimension_semantics=sem, vmem_limit_bytes=VMEM_LIMIT)


def _resident(shape, index_map):
    return pl.BlockSpec(shape, index_map, pipeline_mode=pl.Buffered(1))


def _rms(x, g):
    return x * lax.rsqrt(jnp.mean(x * x, axis=-1, keepdims=True) + EPS) * g


def _tables_kernel(pos_ref, inv_ref, sign_ref, cos_ref, sin_ref, keep_ref):
    inv = inv_ref[...]
    sign = sign_ref[...]
    for r in range(pos_ref.shape[0]):
        p = pos_ref[r:r + 1, :].astype(F32)
        ang_t = inv * p
        rows = pl.ds(r * LANES, LANES)
        cos_ref[rows, :] = jnp.cos(ang_t).T
        sin_ref[rows, :] = (jnp.sin(ang_t) * sign).T
        keep_ref[rows, :] = jnp.where(jnp.broadcast_to(p, (LANES, LANES)) == 0.0, 0.0, 1.0).T


def _rope_tables(positions):
    n = positions.size
    rows = SUBLANES
    pos2 = positions.reshape(n // LANES, LANES)
    half = HEAD_DIM // 2
    inv = 1.0 / (ROPE_THETA ** (jnp.arange(0, HEAD_DIM, 2, dtype=F32) / HEAD_DIM))
    inv_col = jnp.tile(inv, LANES // half).reshape(LANES, 1)
    sign_col = jnp.tile(jnp.concatenate([-jnp.ones(half, F32), jnp.ones(half, F32)]),
                        LANES // HEAD_DIM).reshape(LANES, 1)
    tab = jax.ShapeDtypeStruct((n, LANES), F32)
    out_spec = pl.BlockSpec((rows * LANES, LANES), lambda i: (i, 0))
    return pl.pallas_call(
        _tables_kernel,
        grid=(n // (rows * LANES),),
        in_specs=[pl.BlockSpec((rows, LANES), lambda i: (i, 0)),
                  pl.BlockSpec((LANES, 1), lambda i: (0, 0)),
                  pl.BlockSpec((LANES, 1), lambda i: (0, 0))],
        out_specs=[out_spec, out_spec, out_spec],
        out_shape=[tab, tab, tab],
        compiler_params=_params(("arbitrary",)),
        name="rope_tables",
    )(pos2, inv_col, sign_col)


def _ffn_kernel(*refs, mixed, final):
    if mixed:
        x_in_ref, a_ref, b_ref, c_ref, wo_ref, *refs, x_ref = refs
        mix = jnp.concatenate([a_ref[...]] + [b_ref[s] for s in range(B_SLABS)] + [c_ref[...]], axis=1)
        x_ref[...] = x_in_ref[...] + jnp.dot(mix, wo_ref[...], preferred_element_type=F32)
    else:
        x_ref, *refs = refs
    g_ref, wg_ref, wu_ref, wd_ref, gf_ref, o_ref, xn_ref, acc_ref = refs
    xn_ref[...] = _rms(x_ref[...], g_ref[...]).astype(BF16)
    for c in range(N_FF_CHUNKS):
        xn = xn_ref[...]
        cols = slice(c * FF_CHUNK, (c + 1) * FF_CHUNK)
        g = jnp.dot(xn, wg_ref[:, cols], preferred_element_type=F32)
        u = jnp.dot(xn, wu_ref[:, cols], preferred_element_type=F32)
        h = (g * (1.0 / (1.0 + jnp.exp(-g))) * u).astype(BF16)
        d = jnp.dot(h, wd_ref[cols, :], preferred_element_type=F32)
        if c == 0:
            acc_ref[...] = d
        elif c < N_FF_CHUNKS - 1:
            acc_ref[...] += d
        else:
            y = x_ref[...] + 0.5 * (acc_ref[...] + d)
            if final:
                y = _rms(y, gf_ref[...])
            o_ref[...] = y


def _ffn(x, g, wg, wu, wd, gf, layer, *, final, tm, mix=None):
    n = x.shape[0]
    row = lambda width: pl.BlockSpec((tm, width), lambda i: (i, 0))
    per_layer = lambda rows, cols: _resident((None, rows, cols), lambda i: (layer, 0, 0))
    in_specs = [row(D_MODEL)]
    args = [x]
    scratch = [pltpu.VMEM((tm, D_MODEL), BF16), pltpu.VMEM((tm, D_MODEL), F32)]
    if mix is not None:
        in_specs += [row(A_WIDTH), pl.BlockSpec((B_SLABS, tm, LANES), lambda i: (0, i, 0)),
                     row(C_WIDTH), per_layer(MIX_WIDTH, D_MODEL)]
        args += list(mix)
        scratch.append(pltpu.VMEM((tm, D_MODEL), F32))
    in_specs += [per_layer(1, D_MODEL), per_layer(D_MODEL, D_FF), per_layer(D_MODEL, D_FF),
                 per_layer(D_FF, D_MODEL), _resident((1, D_MODEL), lambda i: (0, 0))]
    args += [g, wg, wu, wd, gf]
    return pl.pallas_call(
        functools.partial(_ffn_kernel, mixed=mix is not None, final=final),
        grid=(n // tm,),
        in_specs=in_specs,
        out_specs=row(D_MODEL),
        out_shape=jax.ShapeDtypeStruct((n, D_MODEL), F32),
        scratch_shapes=scratch,
        compiler_params=_params(("parallel",)),
        name="ffn_mix" if mix is not None else "ffn",
    )(*args)


def _rope(v, cos, sin_signed, first_half):
    rot = jnp.where(first_half, pltpu.roll(v, LANES - HEAD_DIM // 2, axis=1),
                    pltpu.roll(v, HEAD_DIM // 2, axis=1))
    return v * cos + rot * sin_signed


def _inproj_kernel(x_ref, g_ref, w_ref, cos_ref, sin_ref,
                   qa_ref, ka_ref, va_ref, qb_ref, kb_ref, vb_ref, q4_ref, k4_ref, v4_ref,
                   xc_ref, gc_ref, tmp_ref):
    xn = _rms(x_ref[...], g_ref[...]).astype(BF16)
    cos = cos_ref[...]
    sin = sin_ref[...]
    first_half = lax.broadcasted_iota(jnp.int32, cos.shape, 1) % HEAD_DIM < HEAD_DIM // 2

    group = 4 * LANES
    groups = [jnp.dot(xn, w_ref[:, c:min(c + group, IN_COLS)], preferred_element_type=F32)
              for c in range(0, IN_COLS, group)]

    def proj(col, width):
        parts = [groups[c // group][:, c % group:c % group + LANES] for c in range(col, col + width, LANES)]
        return parts[0] if len(parts) == 1 else jnp.concatenate(parts, axis=1)

    col = 0
    for s in range(A_SLABS):
        qa_ref[:, s * LANES:(s + 1) * LANES] = _rope(proj(col, LANES), cos, sin, first_half).astype(BF16)
        col += LANES
    ka_ref[...] = _rope(proj(col, LANES), cos, sin, first_half).astype(BF16)
    col += LANES
    va_ref[...] = proj(col, LANES).astype(BF16)
    col += LANES
    sub = x_ref.shape[0] // B_SPLIT
    for t, (ref, split_ref, rotary) in enumerate(((qb_ref, q4_ref, True), (kb_ref, k4_ref, True),
                                                  (vb_ref, v4_ref, False))):
        for s in range(B_SLABS):
            v = proj(col, LANES)
            if rotary:
                v = _rope(v, cos, sin, first_half)
            ref[s] = v.astype(BF16)
            tmp_ref[t * B_SLABS + s] = v
            for c in range(B_SPLIT):
                split_ref[s, c] = tmp_ref[t * B_SLABS + s, pl.ds(c, sub, stride=B_SPLIT), :]
            col += LANES
    xc_ref[...] = proj(col, C_WIDTH)
    col += C_WIDTH
    gc_ref[...] = proj(col, C_WIDTH)


def _inproj(x, g, w, cos, sin, layer, *, tm):
    n = x.shape[0]
    row = lambda width: pl.BlockSpec((tm, width), lambda i: (i, 0))
    slab = pl.BlockSpec((B_SLABS, tm, LANES), lambda i: (0, i, 0))
    per_tile = B_TILE // tm
    split = pl.BlockSpec((B_SLABS, None, B_SPLIT, tm // B_SPLIT, LANES),
                         lambda i: (0, i // per_tile, 0, i % per_tile, 0))
    sds = jax.ShapeDtypeStruct
    nat = sds((B_SLABS, n, LANES), BF16)
    cls = sds((B_SLABS, n // B_TILE, B_SPLIT, B_TILE // B_SPLIT, LANES), F32)
    return pl.pallas_call(
        _inproj_kernel,
        grid=(n // tm,),
        in_specs=[row(D_MODEL),
                  _resident((None, 1, D_MODEL), lambda i: (layer, 0, 0)),
                  _resident((None, D_MODEL, IN_COLS), lambda i: (layer, 0, 0)),
                  row(LANES), row(LANES)],
        out_specs=[row(A_WIDTH), row(A_KV_WIDTH), row(A_KV_WIDTH), slab, slab, slab,
                   split, split, split, row(C_WIDTH), row(C_WIDTH)],
        out_shape=[sds((n, A_WIDTH), BF16), sds((n, A_KV_WIDTH), BF16), sds((n, A_KV_WIDTH), BF16),
                   nat, nat, nat, cls, cls, cls,
                   sds((n, C_WIDTH), F32), sds((n, C_WIDTH), F32)],
        scratch_shapes=[pltpu.VMEM((3 * B_SLABS, tm, LANES), F32)],
        compiler_params=_params(("parallel",)),
        name="inproj",
    )(x, g, w, cos, sin)


def _band_bias(max_dist):
    row = lax.broadcasted_iota(jnp.int32, (2 * BLOCK, 2 * BLOCK), 0) % BLOCK
    col = lax.broadcasted_iota(jnp.int32, (2 * BLOCK, 2 * BLOCK), 1)
    first_col = row + (BLOCK - max_dist)
    out = []
    for lo in (first_col, jnp.maximum(first_col, BLOCK)):
        out.append(jnp.where((col >= lo) & (col <= row + BLOCK), 0.0, -jnp.inf).astype(F32))
    return jnp.stack(out)


def _attn_unit(q, kcat, vcat, bias):
    lane = lax.broadcasted_iota(jnp.int32, (BLOCK, LANES), 1)
    left = lane < HEAD_DIM
    zero = jnp.zeros_like(q)
    q2 = jnp.concatenate([jnp.where(left, q, zero), jnp.where(left, zero, q)], axis=0)
    s = lax.dot_general(q2, kcat, (((1,), (1,)), ((), ())), preferred_element_type=F32) + bias
    m = jnp.max(s, axis=-1, keepdims=True)
    p = jnp.exp(s - m).astype(BF16)
    v1 = jnp.concatenate([vcat, jnp.ones_like(vcat)], axis=1)
    r = jnp.dot(p, v1, preferred_element_type=F32)
    num = jnp.where(left, r[:BLOCK, :LANES], r[BLOCK:, :LANES])
    den = jnp.where(left, r[:BLOCK, LANES:], r[BLOCK:, LANES:])
    mx = jnp.where(left, jnp.broadcast_to(m[:BLOCK], (BLOCK, LANES)),
                   jnp.broadcast_to(m[BLOCK:], (BLOCK, LANES)))
    return num, den, mx


def _attn_a_kernel(q_ref, kc_ref, kp_ref, vc_ref, vp_ref, sink_ref, bias_ref, o_ref, *, tq):
    no_prev = jnp.where(pl.program_id(1) == 0, 1, 0)
    for j in range(tq // BLOCK):
        rows = slice(j * BLOCK, (j + 1) * BLOCK)
        if j == 0:
            kprev, vprev = kp_ref[...], vp_ref[...]
            bias = bias_ref[no_prev]
        else:
            prev_rows = slice((j - 1) * BLOCK, j * BLOCK)
            kprev, vprev = kc_ref[prev_rows, :], vc_ref[prev_rows, :]
            bias = bias_ref[0]
        kcat = jnp.concatenate([kprev, kc_ref[rows, :]], axis=0)
        vcat = jnp.concatenate([vprev, vc_ref[rows, :]], axis=0)
        for s in range(A_SLABS):
            lanes = slice(s * LANES, (s + 1) * LANES)
            num, den, mx = _attn_unit(q_ref[rows, lanes], kcat, vcat, bias)
            o_ref[rows, lanes] = (num / (den + jnp.exp(sink_ref[s:s + 1, :] - mx))).astype(BF16)


def _attn_a(qa, ka, va, sinks, *, batch, seq, tq):
    tiles = seq // tq
    ratio = tq // BLOCK
    cur = lambda width: pl.BlockSpec((tq, width), lambda b, i: (b * tiles + i, 0))
    prev = pl.BlockSpec((BLOCK, A_KV_WIDTH),
                        lambda b, i: (jnp.maximum((b * tiles + i) * ratio - 1, 0), 0))
    return pl.pallas_call(
        functools.partial(_attn_a_kernel, tq=tq),
        grid=(batch, tiles),
        in_specs=[cur(A_WIDTH), cur(A_KV_WIDTH), prev, cur(A_KV_WIDTH), prev,
                  _resident((A_SLABS, LANES), lambda b, i: (0, 0)),
                  _resident((2, 2 * BLOCK, 2 * BLOCK), lambda b, i: (0, 0, 0))],
        out_specs=cur(A_WIDTH),
        out_shape=jax.ShapeDtypeStruct((batch * seq, A_WIDTH), BF16),
        compiler_params=_params(("parallel", "parallel")),
        name="attn_a",
    )(qa, ka, ka, va, va, sinks, _band_bias(A_WINDOW - 1))


def _attn_b_kernel(qn_ref, kn_ref, knp_ref, vn_ref, vnp_ref, q4_ref, k4_ref, k4p_ref, v4_ref, v4p_ref,
                   bias_ref, o_ref, num_ref, den_ref, mx_ref, cls_ref):
    no_prev = jnp.where(pl.program_id(1) == 0, 1, 0)
    nbr = len(B_BRANCHES)
    sub = B_TILE // B_SPLIT
    results = (num_ref, den_ref, mx_ref)

    def run(q, kprev, kcur, vprev, vcur, bias):
        return _attn_unit(q.astype(BF16), jnp.concatenate([kprev, kcur], axis=0).astype(BF16),
                          jnp.concatenate([vprev, vcur], axis=0).astype(BF16), bias)

    for br, (window, d) in enumerate(B_BRANCHES):
        assert window // d == BLOCK
        if d == 1:
            kprev, vprev = knp_ref[...], vnp_ref[...]
            for j in range(B_TILE // BLOCK):
                rows = slice(j * BLOCK, (j + 1) * BLOCK)
                kcur, vcur = kn_ref[rows, :], vn_ref[rows, :]
                res = run(qn_ref[rows, :], kprev, kcur, vprev, vcur, bias_ref[no_prev if j == 0 else 0])
                for ref, val in zip(results, res):
                    ref[br, rows, :] = val
                kprev, vprev = kcur, vcur
        elif d == B_SPLIT:
            for c in range(B_SPLIT):
                kprev, vprev = k4p_ref[c, sub - BLOCK:sub, :], v4p_ref[c, sub - BLOCK:sub, :]
                for j in range(sub // BLOCK):
                    rows = slice(j * BLOCK, (j + 1) * BLOCK)
                    kcur, vcur = k4_ref[c, rows, :], v4_ref[c, rows, :]
                    res = run(q4_ref[c, rows, :], kprev, kcur, vprev, vcur,
                              bias_ref[no_prev if j == 0 else 0])
                    for ref, val in zip(results, res):
                        ref[br, pl.ds(j * BLOCK * d + c, BLOCK, stride=d), :] = val
                    kprev, vprev = kcur, vcur
        else:
            inner = d // B_SPLIT
            assert d % B_SPLIT == 0 and sub == BLOCK * inner
            for c in range(B_SPLIT):
                for ri in range(inner):
                    rows = pl.ds(ri, BLOCK, stride=inner)
                    res = run(q4_ref[c, rows, :], k4p_ref[c, rows, :], k4_ref[c, rows, :],
                              v4p_ref[c, rows, :], v4_ref[c, rows, :], bias_ref[no_prev])
                    for a, val in enumerate(res):
                        cls_ref[a, c, rows, :] = val
            for a, ref in enumerate(results):
                for c in range(B_SPLIT):
                    ref[br, pl.ds(c, sub, stride=B_SPLIT), :] = cls_ref[a, c]

    def merge(c, carry):
        rows = pl.ds(pl.multiple_of(c * BLOCK, BLOCK), BLOCK)
        mxs = [mx_ref[br, rows, :] for br in range(nbr)]
        top = functools.reduce(jnp.maximum, mxs)
        es = [jnp.exp(x - top) for x in mxs]
        num = functools.reduce(jnp.add, [num_ref[br, rows, :] * es[br] for br in range(nbr)])
        den = functools.reduce(jnp.add, [den_ref[br, rows, :] * es[br] for br in range(nbr)])
        o_ref[rows, :] = (num / den).astype(BF16)
        return carry

    lax.fori_loop(0, B_TILE // BLOCK, merge, 0, unroll=2)


def _attn_b(qn, kn, vn, q4, k4, v4, *, batch, seq):
    tiles = seq // B_TILE
    sub = B_TILE // B_SPLIT
    tile = lambda b, i: b * tiles + i
    nat = pl.BlockSpec((None, B_TILE, LANES), lambda b, i, s: (s, tile(b, i), 0))
    halo = pl.BlockSpec((None, BLOCK, LANES),
                        lambda b, i, s: (s, jnp.maximum(tile(b, i) * (B_TILE // BLOCK) - 1, 0), 0))
    cls = pl.BlockSpec((None, None, B_SPLIT, sub, LANES), lambda b, i, s: (s, tile(b, i), 0, 0, 0))
    cls_prev = pl.BlockSpec((None, None, B_SPLIT, sub, LANES),
                            lambda b, i, s: (s, jnp.maximum(tile(b, i) - 1, 0), 0, 0, 0))
    nbr = len(B_BRANCHES)
    return pl.pallas_call(
        _attn_b_kernel,
        grid=(batch, tiles, B_SLABS),
        in_specs=[nat, nat, halo, nat, halo, cls, cls, cls_prev, cls, cls_prev,
                  _resident((2, 2 * BLOCK, 2 * BLOCK), lambda b, i, s: (0, 0, 0))],
        out_specs=nat,
        out_shape=jax.ShapeDtypeStruct((B_SLABS, batch * seq, LANES), BF16),
        scratch_shapes=[pltpu.VMEM((nbr, B_TILE, LANES), F32)] * 3
                       + [pltpu.VMEM((3, B_SPLIT, sub, LANES), F32)],
        compiler_params=_params(("parallel", "parallel", "parallel")),
        name="attn_b",
    )(qn, kn, kn, vn, vn, q4, k4, k4, v4, v4, _band_bias(BLOCK))


def _rglru_kernel(xc_ref, xh_ref, gc_ref, keep_ref, cw_ref, cb_ref, wg_ref, bg_ref, lam_ref,
                  o_ref, xbuf_ref, a_ref, b_ref, carry_ref, *, tm):
    first_tile = pl.program_id(1) == 0

    @pl.when(first_tile)
    def _():
        carry_ref[...] = jnp.zeros_like(carry_ref)

    xbuf_ref[0:SUBLANES, :] = jnp.where(first_tile, 0.0, xh_ref[...])
    xbuf_ref[SUBLANES:, :] = xc_ref[...]
    y = cb_ref[...] + cw_ref[0:1, :] * xc_ref[...]
    for j in range(1, C_CONV):
        y = y + cw_ref[j:j + 1, :] * xbuf_ref[pl.ds(SUBLANES - j, tm), :]

    gates = jnp.dot(y.astype(BF16), wg_ref[...], preferred_element_type=F32) + bg_ref[...]
    gates = 1.0 / (1.0 + jnp.exp(-gates))
    r, ig = gates[:, :C_WIDTH], gates[:, C_WIDTH:]
    lam = lam_ref[...]
    softplus = jnp.maximum(-lam, 0.0) + jnp.log(1.0 + jnp.exp(-jnp.abs(lam)))
    log_a = -C_EXP * r * softplus
    keep = jnp.concatenate([keep_ref[...]] * (C_WIDTH // LANES), axis=1) != 0.0
    a = jnp.where(keep, jnp.exp(log_a), 0.0)
    mult = jnp.where(keep, jnp.sqrt(1.0 - jnp.exp(2.0 * log_a)), 1.0)
    b = mult * (ig * y)

    in_group = lax.broadcasted_iota(jnp.int32, a.shape, 0) % SUBLANES
    sh = 1
    while sh < SUBLANES:
        ok = in_group >= sh
        b = b + a * jnp.where(ok, pltpu.roll(b, sh, axis=0), 0.0)
        a = a * jnp.where(ok, pltpu.roll(a, sh, axis=0), 1.0)
        sh *= 2
    a_ref[...] = a
    b_ref[...] = b

    def group(g, carry):
        rows = pl.ds(pl.multiple_of(g * SUBLANES, SUBLANES), SUBLANES)
        h = b_ref[rows, :] + a_ref[rows, :] * carry
        b_ref[rows, :] = h
        return h[SUBLANES - 1:SUBLANES, :]

    carry_ref[...] = lax.fori_loop(0, tm // SUBLANES, group, carry_ref[...])

    gate = gc_ref[...]
    gelu = 0.5 * gate * (1.0 + jnp.tanh(0.7978845608028654 * (gate + 0.044715 * gate * gate * gate)))
    o_ref[...] = (b_ref[...] * gelu).astype(BF16)


def _rglru(xc, gc, keep, cw, cb, wg, bg, lam, layer, *, batch, seq, tm):
    tiles = seq // tm
    ratio = tm // SUBLANES
    row = lambda width: pl.BlockSpec((tm, width), lambda b, i: (b * tiles + i, 0))
    halo = pl.BlockSpec((SUBLANES, C_WIDTH),
                        lambda b, i: (jnp.maximum((b * tiles + i) * ratio - 1, 0), 0))
    per_layer = lambda rows, cols: _resident((None, rows, cols), lambda b, i: (layer, 0, 0))
    return pl.pallas_call(
        functools.partial(_rglru_kernel, tm=tm),
        grid=(batch, tiles),
        in_specs=[row(C_WIDTH), halo, row(C_WIDTH), row(LANES),
                  per_layer(C_CONV, C_WIDTH), per_layer(1, C_WIDTH),
                  per_layer(C_WIDTH, 2 * C_WIDTH), per_layer(1, 2 * C_WIDTH), per_layer(1, C_WIDTH)],
        out_specs=row(C_WIDTH),
        out_shape=jax.ShapeDtypeStruct((batch * seq, C_WIDTH), BF16),
        scratch_shapes=[pltpu.VMEM((tm + SUBLANES, C_WIDTH), F32),
                        pltpu.VMEM((tm, C_WIDTH), F32), pltpu.VMEM((tm, C_WIDTH), F32),
                        pltpu.VMEM((1, C_WIDTH), F32)],
        compiler_params=_params(("arbitrary", "arbitrary")),
        name="rglru",
    )(xc, xc, gc, keep, cw, cb, wg, bg, lam)


def _a_head_order():
    group = A_Q_HEADS // A_KV_HEADS
    return [kv * group + m for m in range(group) for kv in range(A_KV_HEADS)]


def _a_column_perm():
    cols = []
    for h in _a_head_order():
        cols.extend(range(h * HEAD_DIM, (h + 1) * HEAD_DIM))
    return jnp.array(cols, dtype=jnp.int32)


def _block_diag(w):
    depth, blocks, n, _ = w.shape
    eye = jnp.eye(blocks, dtype=w.dtype)
    return jnp.einsum('lhij,hg->lhigj', w, eye).reshape(depth, blocks * n, blocks * n)


def kernel(x, positions, norm_ffn1, ffn1_gate, ffn1_up, ffn1_down, norm_mix, w_in, attn_sinks,
           conv_w, conv_b, rg_w_r, rg_b_r, rg_w_i, rg_b_i, rg_lambda, w_out,
           norm_ffn2, ffn2_gate, ffn2_up, ffn2_down, norm_final):
    batch, seq, _ = x.shape
    depth = w_in.shape[0]
    n = batch * seq
    assert seq % B_TILE == 0
    tm = 512

    perm = _a_column_perm()
    q_cols = jnp.concatenate([w_in[:, :, :A_WIDTH][:, :, perm] * SCALE,
                              w_in[:, :, A_WIDTH:A_WIDTH + 2 * A_KV_WIDTH],
                              w_in[:, :, A_WIDTH + 2 * A_KV_WIDTH:A_WIDTH + 2 * A_KV_WIDTH + B_WIDTH] * SCALE,
                              w_in[:, :, A_WIDTH + 2 * A_KV_WIDTH + B_WIDTH:]], axis=2)
    w_in_b = q_cols.astype(BF16)
    w_out_b = jnp.concatenate([w_out[:, :A_WIDTH][:, perm], w_out[:, A_WIDTH:]], axis=1).astype(BF16)
    sinks = jnp.repeat(attn_sinks[:, jnp.array(_a_head_order())], HEAD_DIM, axis=1)
    sinks = sinks.reshape(depth, A_SLABS, LANES)
    w_gates = jnp.concatenate([_block_diag(rg_w_r), _block_diag(rg_w_i)], axis=2).astype(BF16)
    b_gates = jnp.concatenate([rg_b_r.reshape(depth, 1, C_WIDTH), rg_b_i.reshape(depth, 1, C_WIDTH)], axis=2)
    ffn_w = [(g.astype(BF16), u.astype(BF16), d.astype(BF16))
             for g, u, d in ((ffn1_gate, ffn1_up, ffn1_down), (ffn2_gate, ffn2_up, ffn2_down))]
    vec = lambda v: v.reshape(depth, 1, -1)
    gf = norm_final.reshape(1, D_MODEL)

    cos, sin, keep = _rope_tables(positions)
    h = x.reshape(n, D_MODEL)
    for l in range(depth):
        h = _ffn(h, vec(norm_ffn1), *ffn_w[0], gf, l, final=False, tm=tm)
        qa, ka, va, qb, kb, vb, q4, k4, v4, xc, gc = _inproj(h, vec(norm_mix), w_in_b, cos, sin, l, tm=tm)
        oa = _attn_a(qa, ka, va, sinks[l], batch=batch, seq=seq, tq=tm)
        ob = _attn_b(qb, kb, vb, q4, k4, v4, batch=batch, seq=seq)
        oc = _rglru(xc, gc, keep, conv_w, vec(conv_b), w_gates, b_gates, vec(rg_lambda), l,
                    batch=batch, seq=seq, tm=tm)
        h = _ffn(h, vec(norm_ffn2), *ffn_w[1], gf, l, final=(l == depth - 1), tm=tm,
                 mix=(oa, ob, oc, w_out_b))
    return h.reshape(batch, seq, D_MODEL)
```

```python
import functools

import jax
import jax.numpy as jnp
from jax import lax
from jax.experimental import pallas as pl
from jax.experimental.pallas import tpu as pltpu

D_MODEL = 1024
HEAD_DIM = 64
A_Q_HEADS = 4
A_KV_HEADS = 2
A_WINDOW = 128
B_HEADS = 6
B_BRANCHES = ((128, 1), (512, 4), (2048, 16))
C_WIDTH = 384
C_BLOCKS = 6
C_CONV = 4
C_EXP = 8.0
D_FF = 2816
BLOCK = 128
ROPE_THETA = 10000.0
EPS = 1e-6
SCALE = HEAD_DIM ** -0.5

A_WIDTH = A_Q_HEADS * HEAD_DIM
A_KV_WIDTH = A_KV_HEADS * HEAD_DIM
B_WIDTH = B_HEADS * HEAD_DIM
MIX_WIDTH = A_WIDTH + B_WIDTH + C_WIDTH
IN_COLS = A_WIDTH + 2 * A_KV_WIDTH + 3 * B_WIDTH + 2 * C_WIDTH

LANES = 128
SUBLANES = 8
FF_CHUNK = 256
N_FF_CHUNKS = D_FF // FF_CHUNK
B_SLABS = B_WIDTH // LANES
A_SLABS = A_WIDTH // LANES
MAX_DIL = max(d for _, d in B_BRANCHES)
B_TILE = BLOCK * MAX_DIL
B_SPLIT = 4
VMEM_LIMIT = 56 * 1024 * 1024

F32 = jnp.float32
BF16 = jnp.bfloat16


def _params(sem):
    return pltpu.CompilerParams(dimension_semantics=sem, vmem_limit_bytes=VMEM_LIMIT)


def _resident(shape, index_map):
    return pl.BlockSpec(shape, index_map, pipeline_mode=pl.Buffered(1))


def _rms(x, g):
    return x * lax.rsqrt(jnp.mean(x * x, axis=-1, keepdims=True) + EPS) * g


def _tables_kernel(pos_ref, inv_ref, sign_ref, cos_ref, sin_ref, keep_ref):
    inv = inv_ref[...]
    sign = sign_ref[...]
    for r in range(pos_ref.shape[0]):
        p = pos_ref[r:r + 1, :].astype(F32)
        ang_t = inv * p
        rows = pl.ds(r * LANES, LANES)
        cos_ref[rows, :] = jnp.cos(ang_t).T
        sin_ref[rows, :] = (jnp.sin(ang_t) * sign).T
        keep_ref[rows, :] = jnp.where(jnp.broadcast_to(p, (LANES, LANES)) == 0.0, 0.0, 1.0).T


def _rope_tables(positions):
    n = positions.size
    rows = SUBLANES
    pos2 = positions.reshape(n // LANES, LANES)
    half = HEAD_DIM // 2
    inv = 1.0 / (ROPE_THETA ** (jnp.arange(0, HEAD_DIM, 2, dtype=F32) / HEAD_DIM))
    inv_col = jnp.tile(inv, LANES // half).reshape(LANES, 1)
    sign_col = jnp.tile(jnp.concatenate([-jnp.ones(half, F32), jnp.ones(half, F32)]),
                        LANES // HEAD_DIM).reshape(LANES, 1)
    tab = jax.ShapeDtypeStruct((n, LANES), F32)
    out_spec = pl.BlockSpec((rows * LANES, LANES), lambda i: (i, 0))
    return pl.pallas_call(
        _tables_kernel,
        grid=(n // (rows * LANES),),
        in_specs=[pl.BlockSpec((rows, LANES), lambda i: (i, 0)),
                  pl.BlockSpec((LANES, 1), lambda i: (0, 0)),
                  pl.BlockSpec((LANES, 1), lambda i: (0, 0))],
        out_specs=[out_spec, out_spec, out_spec],
        out_shape=[tab, tab, tab],
        compiler_params=_params(("arbitrary",)),
        name="rope_tables",
    )(pos2, inv_col, sign_col)


def _ffn_kernel(*refs, mixed, final):
    if mixed:
        x_in_ref, a_ref, b_ref, c_ref, wo_ref, *refs, x_ref = refs
        mix = jnp.concatenate([a_ref[...]] + [b_ref[s] for s in range(B_SLABS)] + [c_ref[...]], axis=1)
        x_ref[...] = x_in_ref[...] + jnp.dot(mix, wo_ref[...], preferred_element_type=F32)
    else:
        x_ref, *refs = refs
    g_ref, wg_ref, wu_ref, wd_ref, gf_ref, o_ref, xn_ref, acc_ref = refs
    xn_ref[...] = _rms(x_ref[...], g_ref[...]).astype(BF16)
    for c in range(N_FF_CHUNKS):
        xn = xn_ref[...]
        cols = slice(c * FF_CHUNK, (c + 1) * FF_CHUNK)
        g = jnp.dot(xn, wg_ref[:, cols], preferred_element_type=F32)
        u = jnp.dot(xn, wu_ref[:, cols], preferred_element_type=F32)
        h = (g * (1.0 / (1.0 + jnp.exp(-g))) * u).astype(BF16)
        d = jnp.dot(h, wd_ref[cols, :], preferred_element_type=F32)
        if c == 0:
            acc_ref[...] = d
        elif c < N_FF_CHUNKS - 1:
            acc_ref[...] += d
        else:
            y = x_ref[...] + 0.5 * (acc_ref[...] + d)
            if final:
                y = _rms(y, gf_ref[...])
            o_ref[...] = y


def _ffn(x, g, wg, wu, wd, gf, layer, *, final, tm, mix=None):
    n = x.shape[0]
    row = lambda width: pl.BlockSpec((tm, width), lambda i: (i, 0))
    per_layer = lambda rows, cols: _resident((None, rows, cols), lambda i: (layer, 0, 0))
    in_specs = [row(D_MODEL)]
    args = [x]
    scratch = [pltpu.VMEM((tm, D_MODEL), BF16), pltpu.VMEM((tm, D_MODEL), F32)]
    if mix is not None:
        in_specs += [row(A_WIDTH), pl.BlockSpec((B_SLABS, tm, LANES), lambda i: (0, i, 0)),
                     row(C_WIDTH), per_layer(MIX_WIDTH, D_MODEL)]
        args += list(mix)
        scratch.append(pltpu.VMEM((tm, D_MODEL), F32))
    in_specs += [per_layer(1, D_MODEL), per_layer(D_MODEL, D_FF), per_layer(D_MODEL, D_FF),
                 per_layer(D_FF, D_MODEL), _resident((1, D_MODEL), lambda i: (0, 0))]
    args += [g, wg, wu, wd, gf]
    return pl.pallas_call(
        functools.partial(_ffn_kernel, mixed=mix is not None, final=final),
        grid=(n // tm,),
        in_specs=in_specs,
        out_specs=row(D_MODEL),
        out_shape=jax.ShapeDtypeStruct((n, D_MODEL), F32),
        scratch_shapes=scratch,
        compiler_params=_params(("parallel",)),
        name="ffn_mix" if mix is not None else "ffn",
    )(*args)


def _rope(v, cos, sin_signed, first_half):
    rot = jnp.where(first_half, pltpu.roll(v, LANES - HEAD_DIM // 2, axis=1),
                    pltpu.roll(v, HEAD_DIM // 2, axis=1))
    return v * cos + rot * sin_signed


def _inproj_kernel(x_ref, g_ref, w_ref, cos_ref, sin_ref, keep_ref, cw_ref, cb_ref, wgate_ref,
                   bgate_ref, lam_ref,
                   qa_ref, ka_ref, va_ref, qb_ref, kb_ref, vb_ref, q4_ref, k4_ref, v4_ref, oc_ref,
                   tmp_ref, xbuf_ref, a_ref, b_ref, cin_ref, pc_ref, *, tiles_per_seq):
    step = pl.program_id(0)

    @pl.when(step == 0)
    def _():
        pc_ref[...] = jnp.zeros_like(pc_ref)
        cin_ref[...] = jnp.zeros_like(cin_ref)
        xbuf_ref[...] = jnp.zeros_like(xbuf_ref)

    @pl.when((step + tiles_per_seq - 1) % tiles_per_seq == 0)
    def _():
        cin_ref[...] = jnp.zeros_like(cin_ref)
        xbuf_ref[0:SUBLANES, :] = jnp.zeros((SUBLANES, C_WIDTH), F32)

    oc_ref[...] = _rglru_tile(pc_ref[:, :C_WIDTH], pc_ref[:, C_WIDTH:], keep_ref[...],
                              cw_ref, cb_ref, wgate_ref, bgate_ref, lam_ref,
                              xbuf_ref, a_ref, b_ref, cin_ref).astype(BF16)

    xn = _rms(x_ref[...], g_ref[...]).astype(BF16)
    cos = cos_ref[...]
    sin = sin_ref[...]
    first_half = lax.broadcasted_iota(jnp.int32, cos.shape, 1) % HEAD_DIM < HEAD_DIM // 2

    group = 4 * LANES
    groups = [jnp.dot(xn, w_ref[:, c:min(c + group, IN_COLS)], preferred_element_type=F32)
              for c in range(0, IN_COLS, group)]

    def proj(col, width):
        parts = [groups[c // group][:, c % group:c % group + LANES] for c in range(col, col + width, LANES)]
        return parts[0] if len(parts) == 1 else jnp.concatenate(parts, axis=1)

    pc_ref[...] = proj(0, 2 * C_WIDTH)
    col = 2 * C_WIDTH
    for s in range(A_SLABS):
        qa_ref[:, s * LANES:(s + 1) * LANES] = _rope(proj(col, LANES), cos, sin, first_half).astype(BF16)
        col += LANES
    ka_ref[...] = _rope(proj(col, LANES), cos, sin, first_half).astype(BF16)
    col += LANES
    va_ref[...] = proj(col, LANES).astype(BF16)
    col += LANES
    sub = x_ref.shape[0] // B_SPLIT
    for t, (ref, split_ref, rotary) in enumerate(((qb_ref, q4_ref, True), (kb_ref, k4_ref, True),
                                                  (vb_ref, v4_ref, False))):
        for s in range(B_SLABS):
            v = proj(col, LANES)
            if rotary:
                v = _rope(v, cos, sin, first_half)
            ref[s] = v.astype(BF16)
            tmp_ref[t * B_SLABS + s] = v
            for c in range(B_SPLIT):
                split_ref[s, c] = tmp_ref[t * B_SLABS + s, pl.ds(c, sub, stride=B_SPLIT), :]
            col += LANES


def _rglru_tile(xc, gate, keep, cw_ref, cb_ref, wg_ref, bg_ref, lam_ref,
                xbuf_ref, a_ref, b_ref, cin_ref):
    tm = xc.shape[0]

    xbuf_ref[SUBLANES:, :] = xc
    y = cb_ref[...] + cw_ref[0:1, :] * xc
    for j in range(1, C_CONV):
        y = y + cw_ref[j:j + 1, :] * xbuf_ref[pl.ds(SUBLANES - j, tm), :]
    xbuf_ref[0:SUBLANES, :] = xbuf_ref[tm:tm + SUBLANES, :]

    gates = jnp.dot(y.astype(BF16), wg_ref[...], preferred_element_type=F32) + bg_ref[...]
    gates = 1.0 / (1.0 + jnp.exp(-gates))
    r, ig = gates[:, :C_WIDTH], gates[:, C_WIDTH:]
    lam = lam_ref[...]
    softplus = jnp.maximum(-lam, 0.0) + jnp.log(1.0 + jnp.exp(-jnp.abs(lam)))
    a = jnp.exp(-C_EXP * r * softplus)
    keep = jnp.concatenate([keep] * (C_WIDTH // LANES), axis=1) != 0.0
    mult = jnp.where(keep, jnp.sqrt(1.0 - a * a), 1.0)
    a = jnp.where(keep, a, 0.0)
    b = mult * (ig * y)

    in_group = lax.broadcasted_iota(jnp.int32, a.shape, 0) % SUBLANES
    sh = 1
    while sh < SUBLANES:
        ok = in_group >= sh
        b = b + a * jnp.where(ok, pltpu.roll(b, sh, axis=0), 0.0)
        a = a * jnp.where(ok, pltpu.roll(a, sh, axis=0), 1.0)
        sh *= 2

    groups = tm // SUBLANES
    grp = lax.broadcasted_iota(jnp.int32, (groups, LANES), 0)
    last = SUBLANES - 1
    slabs = []
    for s in range(C_WIDTH // LANES):
        lanes = slice(s * LANES, (s + 1) * LANES)
        a_ref[s] = a[:, lanes]
        b_ref[s] = b[:, lanes]
        a_end = a_ref[s, pl.ds(last, groups, stride=SUBLANES), :]
        b_end = b_ref[s, pl.ds(last, groups, stride=SUBLANES), :]
        b_end = b_end + jnp.where(grp == 0, a_end * cin_ref[s, last:SUBLANES, :], 0.0)
        sh = 1
        while sh < groups:
            ok = grp >= sh
            b_end = b_end + a_end * jnp.where(ok, pltpu.roll(b_end, sh, axis=0), 0.0)
            a_end = a_end * jnp.where(ok, pltpu.roll(a_end, sh, axis=0), 1.0)
            sh *= 2
        cin_ref[s, SUBLANES:, :] = b_end
        hs = []
        for g in range(groups):
            rows = slice(g * SUBLANES, (g + 1) * SUBLANES)
            hs.append(b_ref[s, rows, :] + a_ref[s, rows, :] * cin_ref[s, last + g:SUBLANES + g, :])
        slabs.append(jnp.concatenate(hs, axis=0))
        cin_ref[s, last:SUBLANES, :] = b_end[groups - 1:groups, :]
    h = jnp.concatenate(slabs, axis=1)

    gelu = 0.5 * gate * (1.0 + jnp.tanh(0.7978845608028654 * (gate + 0.044715 * gate * gate * gate)))
    return h * gelu


def _inproj(x, g, w, cos, sin, keep, cw, cb, wgate, bgate, lam, layer, *, tm, seq):
    n = x.shape[0]
    tiles = n // tm
    cur = lambda i: jnp.minimum(i, tiles - 1)
    lag = lambda i: jnp.maximum(i - 1, 0)
    row = lambda width: pl.BlockSpec((tm, width), lambda i: (cur(i), 0))
    lag_row = lambda width: pl.BlockSpec((tm, width), lambda i: (lag(i), 0))
    slab = pl.BlockSpec((B_SLABS, tm, LANES), lambda i: (0, cur(i), 0))
    per_tile = B_TILE // tm
    split = pl.BlockSpec((B_SLABS, None, B_SPLIT, tm // B_SPLIT, LANES),
                         lambda i: (0, cur(i) // per_tile, 0, cur(i) % per_tile, 0))
    sds = jax.ShapeDtypeStruct
    nat = sds((B_SLABS, n, LANES), BF16)
    cls = sds((B_SLABS, n // B_TILE, B_SPLIT, B_TILE // B_SPLIT, LANES), F32)
    per_layer = lambda rows, cols: _resident((None, rows, cols), lambda i: (layer, 0, 0))
    return pl.pallas_call(
        functools.partial(_inproj_kernel, tiles_per_seq=seq // tm),
        grid=(tiles + 1,),
        in_specs=[row(D_MODEL), per_layer(1, D_MODEL), per_layer(D_MODEL, IN_COLS),
                  row(LANES), row(LANES), lag_row(LANES),
                  per_layer(C_CONV, C_WIDTH), per_layer(1, C_WIDTH),
                  per_layer(C_WIDTH, 2 * C_WIDTH), per_layer(1, 2 * C_WIDTH), per_layer(1, C_WIDTH)],
        out_specs=[row(A_WIDTH), row(A_KV_WIDTH), row(A_KV_WIDTH), slab, slab, slab,
                   split, split, split, lag_row(C_WIDTH)],
        out_shape=[sds((n, A_WIDTH), BF16), sds((n, A_KV_WIDTH), BF16), sds((n, A_KV_WIDTH), BF16),
                   nat, nat, nat, cls, cls, cls, sds((n, C_WIDTH), BF16)],
        scratch_shapes=[pltpu.VMEM((3 * B_SLABS, tm, LANES), F32),
                        pltpu.VMEM((tm + SUBLANES, C_WIDTH), F32),
                        pltpu.VMEM((C_WIDTH // LANES, tm, LANES), F32),
                        pltpu.VMEM((C_WIDTH // LANES, tm, LANES), F32),
                        pltpu.VMEM((C_WIDTH // LANES, SUBLANES + tm // SUBLANES, LANES), F32),
                        pltpu.VMEM((tm, 2 * C_WIDTH), F32)],
        compiler_params=_params(("arbitrary",)),
        name="inproj",
    )(x, g, w, cos, sin, keep, cw, cb, wgate, bgate, lam)


def _band_bias(max_dist):
    row = lax.broadcasted_iota(jnp.int32, (2 * BLOCK, 2 * BLOCK), 0) % BLOCK
    col = lax.broadcasted_iota(jnp.int32, (2 * BLOCK, 2 * BLOCK), 1)
    first_col = row + (BLOCK - max_dist)
    out = []
    for lo in (first_col, jnp.maximum(first_col, BLOCK)):
        out.append(jnp.where((col >= lo) & (col <= row + BLOCK), 0.0, -jnp.inf).astype(F32))
    return jnp.stack(out)


def _attn_unit(q, kcat, vcat, bias):
    lane = lax.broadcasted_iota(jnp.int32, (BLOCK, LANES), 1)
    left = lane < HEAD_DIM
    zero = jnp.zeros_like(q)
    q2 = jnp.concatenate([jnp.where(left, q, zero), jnp.where(left, zero, q)], axis=0)
    s = lax.dot_general(q2, kcat, (((1,), (1,)), ((), ())), preferred_element_type=F32) + bias
    m = jnp.max(s, axis=-1, keepdims=True)
    p = jnp.exp(s - m).astype(BF16)
    v1 = jnp.concatenate([vcat, jnp.ones_like(vcat)], axis=1)
    r = jnp.dot(p, v1, preferred_element_type=F32)
    num = jnp.where(left, r[:BLOCK, :LANES], r[BLOCK:, :LANES])
    den = jnp.where(left, r[:BLOCK, LANES:], r[BLOCK:, LANES:])
    mx = jnp.where(left, jnp.broadcast_to(m[:BLOCK], (BLOCK, LANES)),
                   jnp.broadcast_to(m[BLOCK:], (BLOCK, LANES)))
    return num, den, mx


def _attn_a_kernel(q_ref, kc_ref, kp_ref, vc_ref, vp_ref, sink_ref, bias_ref, o_ref, *, tq):
    no_prev = jnp.where(pl.program_id(1) == 0, 1, 0)
    for j in range(tq // BLOCK):
        rows = slice(j * BLOCK, (j + 1) * BLOCK)
        if j == 0:
            kprev, vprev = kp_ref[...], vp_ref[...]
            bias = bias_ref[no_prev]
        else:
            prev_rows = slice((j - 1) * BLOCK, j * BLOCK)
            kprev, vprev = kc_ref[prev_rows, :], vc_ref[prev_rows, :]
            bias = bias_ref[0]
        kcat = jnp.concatenate([kprev, kc_ref[rows, :]], axis=0)
        vcat = jnp.concatenate([vprev, vc_ref[rows, :]], axis=0)
        for s in range(A_SLABS):
            lanes = slice(s * LANES, (s + 1) * LANES)
            num, den, mx = _attn_unit(q_ref[rows, lanes], kcat, vcat, bias)
            o_ref[rows, lanes] = (num / (den + jnp.exp(sink_ref[s:s + 1, :] - mx))).astype(BF16)


def _attn_a(qa, ka, va, sinks, *, batch, seq, tq):
    tiles = seq // tq
    ratio = tq // BLOCK
    cur = lambda width: pl.BlockSpec((tq, width), lambda b, i: (b * tiles + i, 0))
    prev = pl.BlockSpec((BLOCK, A_KV_WIDTH),
                        lambda b, i: (jnp.maximum((b * tiles + i) * ratio - 1, 0), 0))
    return pl.pallas_call(
        functools.partial(_attn_a_kernel, tq=tq),
        grid=(batch, tiles),
        in_specs=[cur(A_WIDTH), cur(A_KV_WIDTH), prev, cur(A_KV_WIDTH), prev,
                  _resident((A_SLABS, LANES), lambda b, i: (0, 0)),
                  _resident((2, 2 * BLOCK, 2 * BLOCK), lambda b, i: (0, 0, 0))],
        out_specs=cur(A_WIDTH),
        out_shape=jax.ShapeDtypeStruct((batch * seq, A_WIDTH), BF16),
        compiler_params=_params(("parallel", "parallel")),
        name="attn_a",
    )(qa, ka, ka, va, va, sinks, _band_bias(A_WINDOW - 1))


def _attn_b_kernel(qn_ref, kn_ref, knp_ref, vn_ref, vnp_ref, q4_ref, k4_ref, k4p_ref, v4_ref, v4p_ref,
                   bias_ref, o_ref, num_ref, den_ref, mx_ref, cls_ref):
    no_prev = jnp.where(pl.program_id(1) == 0, 1, 0)
    nbr = len(B_BRANCHES)
    sub = B_TILE // B_SPLIT
    results = (num_ref, den_ref, mx_ref)

    def run(q, kprev, kcur, vprev, vcur, bias):
        return _attn_unit(q.astype(BF16), jnp.concatenate([kprev, kcur], axis=0).astype(BF16),
                          jnp.concatenate([vprev, vcur], axis=0).astype(BF16), bias)

    for br, (window, d) in enumerate(B_BRANCHES):
        assert window // d == BLOCK
        if d == 1:
            kprev, vprev = knp_ref[...], vnp_ref[...]
            for j in range(B_TILE // BLOCK):
                rows = slice(j * BLOCK, (j + 1) * BLOCK)
                kcur, vcur = kn_ref[rows, :], vn_ref[rows, :]
                res = run(qn_ref[rows, :], kprev, kcur, vprev, vcur, bias_ref[no_prev if j == 0 else 0])
                for ref, val in zip(results, res):
                    ref[br, rows, :] = val
                kprev, vprev = kcur, vcur
        elif d == B_SPLIT:
            for c in range(B_SPLIT):
                kprev, vprev = k4p_ref[c, sub - BLOCK:sub, :], v4p_ref[c, sub - BLOCK:sub, :]
                for j in range(sub // BLOCK):
                    rows = slice(j * BLOCK, (j + 1) * BLOCK)
                    kcur, vcur = k4_ref[c, rows, :], v4_ref[c, rows, :]
                    res = run(q4_ref[c, rows, :], kprev, kcur, vprev, vcur,
                              bias_ref[no_prev if j == 0 else 0])
                    for ref, val in zip(results, res):
                        ref[br, pl.ds(j * BLOCK * d + c, BLOCK, stride=d), :] = val
                    kprev, vprev = kcur, vcur
        else:
            inner = d // B_SPLIT
            assert d % B_SPLIT == 0 and sub == BLOCK * inner
            for c in range(B_SPLIT):
                for ri in range(inner):
                    rows = pl.ds(ri, BLOCK, stride=inner)
                    res = run(q4_ref[c, rows, :], k4p_ref[c, rows, :], k4_ref[c, rows, :],
                              v4p_ref[c, rows, :], v4_ref[c, rows, :], bias_ref[no_prev])
                    for a, val in enumerate(res):
                        cls_ref[a, c, rows, :] = val
            for a, ref in enumerate(results):
                for c in range(B_SPLIT):
                    ref[br, pl.ds(c, sub, stride=B_SPLIT), :] = cls_ref[a, c]

    def merge(c, carry):
        rows = pl.ds(pl.multiple_of(c * BLOCK, BLOCK), BLOCK)
        mxs = [mx_ref[br, rows, :] for br in range(nbr)]
        top = functools.reduce(jnp.maximum, mxs)
        es = [jnp.exp(x - top) for x in mxs]
        num = functools.reduce(jnp.add, [num_ref[br, rows, :] * es[br] for br in range(nbr)])
        den = functools.reduce(jnp.add, [den_ref[br, rows, :] * es[br] for br in range(nbr)])
        o_ref[rows, :] = (num / den).astype(BF16)
        return carry

    lax.fori_loop(0, B_TILE // BLOCK, merge, 0, unroll=2)


def _attn_b(qn, kn, vn, q4, k4, v4, *, batch, seq):
    tiles = seq // B_TILE
    sub = B_TILE // B_SPLIT
    tile = lambda b, i: b * tiles + i
    nat = pl.BlockSpec((None, B_TILE, LANES), lambda b, i, s: (s, tile(b, i), 0))
    halo = pl.BlockSpec((None, BLOCK, LANES),
                        lambda b, i, s: (s, jnp.maximum(tile(b, i) * (B_TILE // BLOCK) - 1, 0), 0))
    cls = pl.BlockSpec((None, None, B_SPLIT, sub, LANES), lambda b, i, s: (s, tile(b, i), 0, 0, 0))
    cls_prev = pl.BlockSpec((None, None, B_SPLIT, sub, LANES),
                            lambda b, i, s: (s, jnp.maximum(tile(b, i) - 1, 0), 0, 0, 0))
    nbr = len(B_BRANCHES)
    return pl.pallas_call(
        _attn_b_kernel,
        grid=(batch, tiles, B_SLABS),
        in_specs=[nat, nat, halo, nat, halo, cls, cls, cls_prev, cls, cls_prev,
                  _resident((2, 2 * BLOCK, 2 * BLOCK), lambda b, i, s: (0, 0, 0))],
        out_specs=nat,
        out_shape=jax.ShapeDtypeStruct((B_SLABS, batch * seq, LANES), BF16),
        scratch_shapes=[pltpu.VMEM((nbr, B_TILE, LANES), F32)] * 3
                       + [pltpu.VMEM((3, B_SPLIT, sub, LANES), F32)],
        compiler_params=_params(("parallel", "parallel", "parallel")),
        name="attn_b",
    )(qn, kn, kn, vn, vn, q4, k4, k4, v4, v4, _band_bias(BLOCK))


def _a_head_order():
    group = A_Q_HEADS // A_KV_HEADS
    return [kv * group + m for m in range(group) for kv in range(A_KV_HEADS)]


def _a_column_perm():
    cols = []
    for h in _a_head_order():
        cols.extend(range(h * HEAD_DIM, (h + 1) * HEAD_DIM))
    return jnp.array(cols, dtype=jnp.int32)


def _block_diag(w):
    depth, blocks, n, _ = w.shape
    eye = jnp.eye(blocks, dtype=w.dtype)
    return jnp.einsum('lhij,hg->lhigj', w, eye).reshape(depth, blocks * n, blocks * n)


def kernel(x, positions, norm_ffn1, ffn1_gate, ffn1_up, ffn1_down, norm_mix, w_in, attn_sinks,
           conv_w, conv_b, rg_w_r, rg_b_r, rg_w_i, rg_b_i, rg_lambda, w_out,
           norm_ffn2, ffn2_gate, ffn2_up, ffn2_down, norm_final):
    batch, seq, _ = x.shape
    depth = w_in.shape[0]
    n = batch * seq
    assert seq % B_TILE == 0
    tm = 512

    perm = _a_column_perm()
    a_end = A_WIDTH + 2 * A_KV_WIDTH
    b_end = a_end + 3 * B_WIDTH
    w_in_b = jnp.concatenate([w_in[:, :, b_end:],
                              w_in[:, :, :A_WIDTH][:, :, perm] * SCALE,
                              w_in[:, :, A_WIDTH:a_end],
                              w_in[:, :, a_end:a_end + B_WIDTH] * SCALE,
                              w_in[:, :, a_end + B_WIDTH:b_end]], axis=2).astype(BF16)
    w_out_b = jnp.concatenate([w_out[:, :A_WIDTH][:, perm], w_out[:, A_WIDTH:]], axis=1).astype(BF16)
    sinks = jnp.repeat(attn_sinks[:, jnp.array(_a_head_order())], HEAD_DIM, axis=1)
    sinks = sinks.reshape(depth, A_SLABS, LANES)
    w_gates = jnp.concatenate([_block_diag(rg_w_r), _block_diag(rg_w_i)], axis=2).astype(BF16)
    b_gates = jnp.concatenate([rg_b_r.reshape(depth, 1, C_WIDTH), rg_b_i.reshape(depth, 1, C_WIDTH)], axis=2)
    ffn_w = [(g.astype(BF16), u.astype(BF16), d.astype(BF16))
             for g, u, d in ((ffn1_gate, ffn1_up, ffn1_down), (ffn2_gate, ffn2_up, ffn2_down))]
    vec = lambda v: v.reshape(depth, 1, -1)
    gf = norm_final.reshape(1, D_MODEL)

    cos, sin, keep = _rope_tables(positions)
    h = x.reshape(n, D_MODEL)
    for l in range(depth):
        h = _ffn(h, vec(norm_ffn1), *ffn_w[0], gf, l, final=False, tm=tm)
        qa, ka, va, qb, kb, vb, q4, k4, v4, oc = _inproj(
            h, vec(norm_mix), w_in_b, cos, sin, keep, conv_w, vec(conv_b), w_gates, b_gates,
            vec(rg_lambda), l, tm=tm, seq=seq)
        oa = _attn_a(qa, ka, va, sinks[l], batch=batch, seq=seq, tq=tm)
        ob = _attn_b(qb, kb, vb, q4, k4, v4, batch=batch, seq=seq)
        h = _ffn(h, vec(norm_ffn2), *ffn_w[1], gf, l, final=(l == depth - 1), tm=tm,
                 mix=(oa, ob, oc, w_out_b))
    return h.reshape(batch, seq, D_MODEL)
```

```python
import functools

import jax
import jax.numpy as jnp
from jax import lax
from jax.experimental import pallas as pl
from jax.experimental.pallas import tpu as pltpu

D_MODEL = 1024
HEAD_DIM = 64
A_Q_HEADS = 4
A_KV_HEADS = 2
A_WINDOW = 128
B_HEADS = 6
B_BRANCHES = ((128, 1), (512, 4), (2048, 16))
C_WIDTH = 384
C_BLOCKS = 6
C_CONV = 4
C_EXP = 8.0
D_FF = 2816
BLOCK = 128
ROPE_THETA = 10000.0
EPS = 1e-6
SCALE = HEAD_DIM ** -0.5

A_WIDTH = A_Q_HEADS * HEAD_DIM
A_KV_WIDTH = A_KV_HEADS * HEAD_DIM
B_WIDTH = B_HEADS * HEAD_DIM
MIX_WIDTH = A_WIDTH + B_WIDTH + C_WIDTH
IN_COLS = A_WIDTH + 2 * A_KV_WIDTH + 3 * B_WIDTH + 2 * C_WIDTH

LANES = 128
SUBLANES = 8
FF_CHUNK = 256
N_FF_CHUNKS = D_FF // FF_CHUNK
B_SLABS = B_WIDTH // LANES
A_SLABS = A_WIDTH // LANES
MAX_DIL = max(d for _, d in B_BRANCHES)
B_TILE = BLOCK * MAX_DIL
B_SPLIT = 4
VMEM_LIMIT = 56 * 1024 * 1024

F32 = jnp.float32
BF16 = jnp.bfloat16


def _params(sem):
    return pltpu.CompilerParams(dimension_semantics=sem, vmem_limit_bytes=VMEM_LIMIT)


def _resident(shape, index_map):
    return pl.BlockSpec(shape, index_map, pipeline_mode=pl.Buffered(1))


def _rms(x, g):
    return x * lax.rsqrt(jnp.mean(x * x, axis=-1, keepdims=True) + EPS) * g


def _tables_kernel(pos_ref, inv_ref, cos_ref, sin_ref, keep_ref):
    inv = inv_ref[...]
    reps = LANES // HEAD_DIM
    for r in range(pos_ref.shape[0]):
        p = pos_ref[r:r + 1, :].astype(F32)
        ang_t = inv * p
        cos_t, sin_t = jnp.cos(ang_t), jnp.sin(ang_t)
        rows = pl.ds(r * LANES, LANES)
        cos_ref[rows, :] = jnp.concatenate([cos_t, cos_t] * reps, axis=0).T
        sin_ref[rows, :] = jnp.concatenate([-sin_t, sin_t] * reps, axis=0).T
        keep_ref[rows, :] = jnp.where(jnp.broadcast_to(p, (LANES, LANES)) == 0.0, 0.0, 1.0).T


def _rope_tables(positions):
    n = positions.size
    rows = SUBLANES
    pos2 = positions.reshape(n // LANES, LANES)
    half = HEAD_DIM // 2
    inv = 1.0 / (ROPE_THETA ** (jnp.arange(0, HEAD_DIM, 2, dtype=F32) / HEAD_DIM))
    tab = jax.ShapeDtypeStruct((n, LANES), F32)
    out_spec = pl.BlockSpec((rows * LANES, LANES), lambda i: (i, 0))
    return pl.pallas_call(
        _tables_kernel,
        grid=(n // (rows * LANES),),
        in_specs=[pl.BlockSpec((rows, LANES), lambda i: (i, 0)),
                  pl.BlockSpec((half, 1), lambda i: (0, 0))],
        out_specs=[out_spec, out_spec, out_spec],
        out_shape=[tab, tab, tab],
        compiler_params=_params(("arbitrary",)),
        name="rope_tables",
    )(pos2, inv.reshape(half, 1))


def _ffn_kernel(*refs, mixed, final):
    if mixed:
        x_in_ref, a_ref, b_ref, c_ref, wo_ref, *refs, x_ref = refs
        mix = jnp.concatenate([a_ref[...]] + [b_ref[s] for s in range(B_SLABS)] + [c_ref[...]], axis=1)
        x_ref[...] = x_in_ref[...] + jnp.dot(mix, wo_ref[...], preferred_element_type=F32)
    else:
        x_ref, *refs = refs
    g_ref, wg_ref, wu_ref, wd_ref, gf_ref, o_ref, xn_ref, acc_ref = refs
    xn_ref[...] = _rms(x_ref[...], g_ref[...]).astype(BF16)
    for c in range(N_FF_CHUNKS):
        xn = xn_ref[...]
        cols = slice(c * FF_CHUNK, (c + 1) * FF_CHUNK)
        g = jnp.dot(xn, wg_ref[:, cols], preferred_element_type=F32)
        u = jnp.dot(xn, wu_ref[:, cols], preferred_element_type=F32)
        h = (g * (1.0 / (1.0 + jnp.exp(-g))) * u).astype(BF16)
        d = jnp.dot(h, wd_ref[cols, :], preferred_element_type=F32)
        if c == 0:
            acc_ref[...] = d
        elif c < N_FF_CHUNKS - 1:
            acc_ref[...] += d
        else:
            y = x_ref[...] + 0.5 * (acc_ref[...] + d)
            if final:
                y = _rms(y, gf_ref[...])
            o_ref[...] = y


def _ffn(x, g, wg, wu, wd, gf, layer, *, final, tm, mix=None):
    n = x.shape[0]
    row = lambda width: pl.BlockSpec((tm, width), lambda i: (i, 0))
    per_layer = lambda rows, cols: _resident((None, rows, cols), lambda i: (layer, 0, 0))
    in_specs = [row(D_MODEL)]
    args = [x]
    scratch = [pltpu.VMEM((tm, D_MODEL), BF16), pltpu.VMEM((tm, D_MODEL), F32)]
    if mix is not None:
        in_specs += [row(A_WIDTH), pl.BlockSpec((B_SLABS, tm, LANES), lambda i: (0, i, 0)),
                     row(C_WIDTH), per_layer(MIX_WIDTH, D_MODEL)]
        args += list(mix)
        scratch.append(pltpu.VMEM((tm, D_MODEL), F32))
    in_specs += [per_layer(1, D_MODEL), per_layer(D_MODEL, D_FF), per_layer(D_MODEL, D_FF),
                 per_layer(D_FF, D_MODEL), _resident((1, D_MODEL), lambda i: (0, 0))]
    args += [g, wg, wu, wd, gf]
    return pl.pallas_call(
        functools.partial(_ffn_kernel, mixed=mix is not None, final=final),
        grid=(n // tm,),
        in_specs=in_specs,
        out_specs=row(D_MODEL),
        out_shape=jax.ShapeDtypeStruct((n, D_MODEL), F32),
        scratch_shapes=scratch,
        compiler_params=_params(("parallel",)),
        name="ffn_mix" if mix is not None else "ffn",
    )(*args)


def _rope(v, cos, sin_signed, first_half):
    rot = jnp.where(first_half, pltpu.roll(v, LANES - HEAD_DIM // 2, axis=1),
                    pltpu.roll(v, HEAD_DIM // 2, axis=1))
    return v * cos + rot * sin_signed


def _inproj_kernel(x_ref, g_ref, w_ref, cos_ref, sin_ref, keep_ref, cw_ref, cb_ref, wgate_ref,
                   bgate_ref, lam_ref,
                   qa_ref, ka_ref, va_ref, qb_ref, kb_ref, vb_ref, q4_ref, k4_ref, v4_ref, oc_ref,
                   tmp_ref, xbuf_ref, a_ref, b_ref, cin_ref, pc_ref, *, tiles_per_seq):
    step = pl.program_id(0)

    @pl.when(step == 0)
    def _():
        pc_ref[...] = jnp.zeros_like(pc_ref)
        cin_ref[...] = jnp.zeros_like(cin_ref)
        xbuf_ref[...] = jnp.zeros_like(xbuf_ref)

    @pl.when((step + tiles_per_seq - 1) % tiles_per_seq == 0)
    def _():
        cin_ref[...] = jnp.zeros_like(cin_ref)
        xbuf_ref[0:SUBLANES, :] = jnp.zeros((SUBLANES, C_WIDTH), F32)

    xc_lag, gate_lag = pc_ref[:, :C_WIDTH], pc_ref[:, C_WIDTH:]

    xn = _rms(x_ref[...], g_ref[...]).astype(BF16)
    cos = cos_ref[...]
    sin = sin_ref[...]
    first_half = lax.broadcasted_iota(jnp.int32, cos.shape, 1) % HEAD_DIM < HEAD_DIM // 2

    group = 4 * LANES
    groups = [jnp.dot(xn, w_ref[:, c:min(c + group, IN_COLS)], preferred_element_type=F32)
              for c in range(0, IN_COLS, group)]

    def proj(col, width):
        parts = [groups[c // group][:, c % group:c % group + LANES] for c in range(col, col + width, LANES)]
        return parts[0] if len(parts) == 1 else jnp.concatenate(parts, axis=1)

    pc_ref[...] = proj(0, 2 * C_WIDTH)
    col = 2 * C_WIDTH
    for s in range(A_SLABS):
        qa_ref[:, s * LANES:(s + 1) * LANES] = _rope(proj(col, LANES), cos, sin, first_half).astype(BF16)
        col += LANES
    ka_ref[...] = _rope(proj(col, LANES), cos, sin, first_half).astype(BF16)
    col += LANES
    va_ref[...] = proj(col, LANES).astype(BF16)
    col += LANES
    sub = x_ref.shape[0] // B_SPLIT
    for t, (ref, split_ref, rotary) in enumerate(((qb_ref, q4_ref, True), (kb_ref, k4_ref, True),
                                                  (vb_ref, v4_ref, False))):
        for s in range(B_SLABS):
            v = proj(col, LANES)
            if rotary:
                v = _rope(v, cos, sin, first_half)
            ref[s] = v.astype(BF16)
            tmp_ref[t * B_SLABS + s] = v
            for c in range(B_SPLIT):
                split_ref[s, c] = tmp_ref[t * B_SLABS + s, pl.ds(c, sub, stride=B_SPLIT), :]
            col += LANES
    oc_ref[...] = _rglru_tile(xc_lag, gate_lag, keep_ref[...],
                              cw_ref, cb_ref, wgate_ref, bgate_ref, lam_ref,
                              xbuf_ref, a_ref, b_ref, cin_ref).astype(BF16)


def _rglru_tile(xc, gate, keep, cw_ref, cb_ref, wg_ref, bg_ref, lam_ref,
                xbuf_ref, a_ref, b_ref, cin_ref):
    tm = xc.shape[0]

    xbuf_ref[SUBLANES:, :] = xc
    y = cb_ref[...] + cw_ref[0:1, :] * xc
    for j in range(1, C_CONV):
        y = y + cw_ref[j:j + 1, :] * xbuf_ref[pl.ds(SUBLANES - j, tm), :]
    xbuf_ref[0:SUBLANES, :] = xbuf_ref[tm:tm + SUBLANES, :]

    gates = jnp.dot(y.astype(BF16), wg_ref[...], preferred_element_type=F32) + bg_ref[...]
    gates = 1.0 / (1.0 + jnp.exp(-gates))
    r, ig = gates[:, :C_WIDTH], gates[:, C_WIDTH:]
    lam = lam_ref[...]
    softplus = jnp.maximum(-lam, 0.0) + jnp.log(1.0 + jnp.exp(-jnp.abs(lam)))
    a = jnp.exp(-C_EXP * r * softplus)
    keep = jnp.concatenate([keep] * (C_WIDTH // LANES), axis=1) != 0.0
    gap = 1.0 - a * a
    root = jnp.where(gap > 0.0, gap * lax.rsqrt(gap), 0.0)
    mult = jnp.where(keep, root, 1.0)
    a = jnp.where(keep, a, 0.0)
    b = mult * (ig * y)

    in_group = lax.broadcasted_iota(jnp.int32, a.shape, 0) % SUBLANES
    sh = 1
    while sh < SUBLANES:
        ok = in_group >= sh
        b = b + a * jnp.where(ok, pltpu.roll(b, sh, axis=0), 0.0)
        a = a * jnp.where(ok, pltpu.roll(a, sh, axis=0), 1.0)
        sh *= 2

    groups = tm // SUBLANES
    grp = lax.broadcasted_iota(jnp.int32, (groups, LANES), 0)
    last = SUBLANES - 1
    slabs = []
    for s in range(C_WIDTH // LANES):
        lanes = slice(s * LANES, (s + 1) * LANES)
        a_ref[s] = a[:, lanes]
        b_ref[s] = b[:, lanes]
        a_end = a_ref[s, pl.ds(last, groups, stride=SUBLANES), :]
        b_end = b_ref[s, pl.ds(last, groups, stride=SUBLANES), :]
        b_end = b_end + jnp.where(grp == 0, a_end * cin_ref[s, last:SUBLANES, :], 0.0)
        sh = 1
        while sh < groups:
            ok = grp >= sh
            b_end = b_end + a_end * jnp.where(ok, pltpu.roll(b_end, sh, axis=0), 0.0)
            a_end = a_end * jnp.where(ok, pltpu.roll(a_end, sh, axis=0), 1.0)
            sh *= 2
        cin_ref[s, SUBLANES:, :] = b_end
        hs = []
        for g in range(groups):
            rows = slice(g * SUBLANES, (g + 1) * SUBLANES)
            hs.append(b_ref[s, rows, :] + a_ref[s, rows, :] * cin_ref[s, last + g:SUBLANES + g, :])
        slabs.append(jnp.concatenate(hs, axis=0))
        cin_ref[s, last:SUBLANES, :] = b_end[groups - 1:groups, :]
    h = jnp.concatenate(slabs, axis=1)

    gelu = 0.5 * gate * (1.0 + jnp.tanh(0.7978845608028654 * (gate + 0.044715 * gate * gate * gate)))
    return h * gelu


def _inproj(x, g, w, cos, sin, keep, cw, cb, wgate, bgate, lam, layer, *, tm, seq):
    n = x.shape[0]
    tiles = n // tm
    cur = lambda i: jnp.minimum(i, tiles - 1)
    lag = lambda i: jnp.maximum(i - 1, 0)
    row = lambda width: pl.BlockSpec((tm, width), lambda i: (cur(i), 0))
    lag_row = lambda width: pl.BlockSpec((tm, width), lambda i: (lag(i), 0))
    slab = pl.BlockSpec((B_SLABS, tm, LANES), lambda i: (0, cur(i), 0))
    per_tile = B_TILE // tm
    split = pl.BlockSpec((B_SLABS, None, B_SPLIT, tm // B_SPLIT, LANES),
                         lambda i: (0, cur(i) // per_tile, 0, cur(i) % per_tile, 0))
    sds = jax.ShapeDtypeStruct
    nat = sds((B_SLABS, n, LANES), BF16)
    cls = sds((B_SLABS, n // B_TILE, B_SPLIT, B_TILE // B_SPLIT, LANES), F32)
    per_layer = lambda rows, cols: _resident((None, rows, cols), lambda i: (layer, 0, 0))
    return pl.pallas_call(
        functools.partial(_inproj_kernel, tiles_per_seq=seq // tm),
        grid=(tiles + 1,),
        in_specs=[row(D_MODEL), per_layer(1, D_MODEL), per_layer(D_MODEL, IN_COLS),
                  row(LANES), row(LANES), lag_row(LANES),
                  per_layer(C_CONV, C_WIDTH), per_layer(1, C_WIDTH),
                  per_layer(C_WIDTH, 2 * C_WIDTH), per_layer(1, 2 * C_WIDTH), per_layer(1, C_WIDTH)],
        out_specs=[row(A_WIDTH), row(A_KV_WIDTH), row(A_KV_WIDTH), slab, slab, slab,
                   split, split, split, lag_row(C_WIDTH)],
        out_shape=[sds((n, A_WIDTH), BF16), sds((n, A_KV_WIDTH), BF16), sds((n, A_KV_WIDTH), BF16),
                   nat, nat, nat, cls, cls, cls, sds((n, C_WIDTH), BF16)],
        scratch_shapes=[pltpu.VMEM((3 * B_SLABS, tm, LANES), F32),
                        pltpu.VMEM((tm + SUBLANES, C_WIDTH), F32),
                        pltpu.VMEM((C_WIDTH // LANES, tm, LANES), F32),
                        pltpu.VMEM((C_WIDTH // LANES, tm, LANES), F32),
                        pltpu.VMEM((C_WIDTH // LANES, SUBLANES + tm // SUBLANES, LANES), F32),
                        pltpu.VMEM((tm, 2 * C_WIDTH), F32)],
        compiler_params=_params(("arbitrary",)),
        name="inproj",
    )(x, g, w, cos, sin, keep, cw, cb, wgate, bgate, lam)


def _band_bias(max_dist):
    row = lax.broadcasted_iota(jnp.int32, (2 * BLOCK, 2 * BLOCK), 0) % BLOCK
    col = lax.broadcasted_iota(jnp.int32, (2 * BLOCK, 2 * BLOCK), 1)
    first_col = row + (BLOCK - max_dist)
    out = []
    for lo in (first_col, jnp.maximum(first_col, BLOCK)):
        out.append(jnp.where((col >= lo) & (col <= row + BLOCK), 0.0, -jnp.inf).astype(F32))
    return jnp.stack(out)


def _attn_unit(q, kcat, vcat, bias):
    lane = lax.broadcasted_iota(jnp.int32, (BLOCK, LANES), 1)
    left = lane < HEAD_DIM
    zero = jnp.zeros_like(q)
    q2 = jnp.concatenate([jnp.where(left, q, zero), jnp.where(left, zero, q)], axis=0)
    s = lax.dot_general(q2, kcat, (((1,), (1,)), ((), ())), preferred_element_type=F32) + bias
    m = jnp.max(s, axis=-1, keepdims=True)
    p = jnp.exp(s - m).astype(BF16)
    v1 = jnp.concatenate([vcat, jnp.ones_like(vcat)], axis=1)
    r = jnp.dot(p, v1, preferred_element_type=F32)
    num = jnp.where(left, r[:BLOCK, :LANES], r[BLOCK:, :LANES])
    den = jnp.where(left, r[:BLOCK, LANES:], r[BLOCK:, LANES:])
    mx = jnp.where(left, jnp.broadcast_to(m[:BLOCK], (BLOCK, LANES)),
                   jnp.broadcast_to(m[BLOCK:], (BLOCK, LANES)))
    return num, den, mx


def _attn_a_kernel(q_ref, kc_ref, kp_ref, vc_ref, vp_ref, sink_ref, bias_ref, o_ref, *, tq):
    no_prev = jnp.where(pl.program_id(1) == 0, 1, 0)
    for j in range(tq // BLOCK):
        rows = slice(j * BLOCK, (j + 1) * BLOCK)
        if j == 0:
            kprev, vprev = kp_ref[...], vp_ref[...]
            bias = bias_ref[no_prev]
        else:
            prev_rows = slice((j - 1) * BLOCK, j * BLOCK)
            kprev, vprev = kc_ref[prev_rows, :], vc_ref[prev_rows, :]
            bias = bias_ref[0]
        kcat = jnp.concatenate([kprev, kc_ref[rows, :]], axis=0)
        vcat = jnp.concatenate([vprev, vc_ref[rows, :]], axis=0)
        for s in range(A_SLABS):
            lanes = slice(s * LANES, (s + 1) * LANES)
            num, den, mx = _attn_unit(q_ref[rows, lanes], kcat, vcat, bias)
            o_ref[rows, lanes] = (num / (den + jnp.exp(sink_ref[s:s + 1, :] - mx))).astype(BF16)


def _attn_a(qa, ka, va, sinks, *, batch, seq, tq):
    tiles = seq // tq
    ratio = tq // BLOCK
    cur = lambda width: pl.BlockSpec((tq, width), lambda b, i: (b * tiles + i, 0))
    prev = pl.BlockSpec((BLOCK, A_KV_WIDTH),
                        lambda b, i: (jnp.maximum((b * tiles + i) * ratio - 1, 0), 0))
    return pl.pallas_call(
        functools.partial(_attn_a_kernel, tq=tq),
        grid=(batch, tiles),
        in_specs=[cur(A_WIDTH), cur(A_KV_WIDTH), prev, cur(A_KV_WIDTH), prev,
                  _resident((A_SLABS, LANES), lambda b, i: (0, 0)),
                  _resident((2, 2 * BLOCK, 2 * BLOCK), lambda b, i: (0, 0, 0))],
        out_specs=cur(A_WIDTH),
        out_shape=jax.ShapeDtypeStruct((batch * seq, A_WIDTH), BF16),
        compiler_params=_params(("parallel", "parallel")),
        name="attn_a",
    )(qa, ka, ka, va, va, sinks, _band_bias(A_WINDOW - 1))


def _attn_b_kernel(qn_ref, kn_ref, knp_ref, vn_ref, vnp_ref, q4_ref, k4_ref, k4p_ref, v4_ref, v4p_ref,
                   bias_ref, o_ref, num_ref, den_ref, mx_ref, cls_ref):
    no_prev = jnp.where(pl.program_id(1) == 0, 1, 0)
    nbr = len(B_BRANCHES)
    sub = B_TILE // B_SPLIT
    results = (num_ref, den_ref, mx_ref)

    def run(q, kprev, kcur, vprev, vcur, bias):
        return _attn_unit(q.astype(BF16), jnp.concatenate([kprev, kcur], axis=0).astype(BF16),
                          jnp.concatenate([vprev, vcur], axis=0).astype(BF16), bias)

    def merge(rows, first):
        parts = [first] + [tuple(ref[br, rows, :] for ref in results) for br in strided]
        top = functools.reduce(jnp.maximum, [p[2] for p in parts])
        es = [jnp.exp(p[2] - top) for p in parts]
        num = functools.reduce(jnp.add, [p[0] * e for p, e in zip(parts, es)])
        den = functools.reduce(jnp.add, [p[1] * e for p, e in zip(parts, es)])
        o_ref[rows, :] = (num / den).astype(BF16)

    order = sorted(range(nbr), key=lambda br: B_BRANCHES[br][1] == 1)
    strided = [br for br in order if B_BRANCHES[br][1] != 1]
    for br in order:
        window, d = B_BRANCHES[br]
        assert window // d == BLOCK
        if d == 1:
            kprev, vprev = knp_ref[...], vnp_ref[...]
            for j in range(B_TILE // BLOCK):
                rows = slice(j * BLOCK, (j + 1) * BLOCK)
                kcur, vcur = kn_ref[rows, :], vn_ref[rows, :]
                merge(rows, run(qn_ref[rows, :], kprev, kcur, vprev, vcur,
                                bias_ref[no_prev if j == 0 else 0]))
                kprev, vprev = kcur, vcur
        elif d == B_SPLIT:
            for c in range(B_SPLIT):
                kprev, vprev = k4p_ref[c, sub - BLOCK:sub, :], v4p_ref[c, sub - BLOCK:sub, :]
                for j in range(sub // BLOCK):
                    rows = slice(j * BLOCK, (j + 1) * BLOCK)
                    kcur, vcur = k4_ref[c, rows, :], v4_ref[c, rows, :]
                    res = run(q4_ref[c, rows, :], kprev, kcur, vprev, vcur,
                              bias_ref[no_prev if j == 0 else 0])
                    for ref, val in zip(results, res):
                        ref[br, pl.ds(j * BLOCK * d + c, BLOCK, stride=d), :] = val
                    kprev, vprev = kcur, vcur
        else:
            inner = d // B_SPLIT
            assert d % B_SPLIT == 0 and sub == BLOCK * inner
            for c in range(B_SPLIT):
                for ri in range(inner):
                    rows = pl.ds(ri, BLOCK, stride=inner)
                    res = run(q4_ref[c, rows, :], k4p_ref[c, rows, :], k4_ref[c, rows, :],
                              v4p_ref[c, rows, :], v4_ref[c, rows, :], bias_ref[no_prev])
                    for a, val in enumerate(res):
                        cls_ref[a, c, rows, :] = val
            for a, ref in enumerate(results):
                for c in range(B_SPLIT):
                    ref[br, pl.ds(c, sub, stride=B_SPLIT), :] = cls_ref[a, c]


def _attn_b(qn, kn, vn, q4, k4, v4, *, batch, seq):
    tiles = seq // B_TILE
    sub = B_TILE // B_SPLIT
    tile = lambda b, i: b * tiles + i
    nat = pl.BlockSpec((None, B_TILE, LANES), lambda b, i, s: (s, tile(b, i), 0))
    halo = pl.BlockSpec((None, BLOCK, LANES),
                        lambda b, i, s: (s, jnp.maximum(tile(b, i) * (B_TILE // BLOCK) - 1, 0), 0))
    cls = pl.BlockSpec((None, None, B_SPLIT, sub, LANES), lambda b, i, s: (s, tile(b, i), 0, 0, 0))
    cls_prev = pl.BlockSpec((None, None, B_SPLIT, sub, LANES),
                            lambda b, i, s: (s, jnp.maximum(tile(b, i) - 1, 0), 0, 0, 0))
    nbr = len(B_BRANCHES)
    return pl.pallas_call(
        _attn_b_kernel,
        grid=(batch, tiles, B_SLABS),
        in_specs=[nat, nat, halo, nat, halo, cls, cls, cls_prev, cls, cls_prev,
                  _resident((2, 2 * BLOCK, 2 * BLOCK), lambda b, i, s: (0, 0, 0))],
        out_specs=nat,
        out_shape=jax.ShapeDtypeStruct((B_SLABS, batch * seq, LANES), BF16),
        scratch_shapes=[pltpu.VMEM((nbr, B_TILE, LANES), F32)] * 3
                       + [pltpu.VMEM((3, B_SPLIT, sub, LANES), F32)],
        compiler_params=_params(("parallel", "parallel", "parallel")),
        name="attn_b",
    )(qn, kn, kn, vn, vn, q4, k4, k4, v4, v4, _band_bias(BLOCK))


def _a_head_order():
    group = A_Q_HEADS // A_KV_HEADS
    return [kv * group + m for m in range(group) for kv in range(A_KV_HEADS)]


def _a_column_perm():
    cols = []
    for h in _a_head_order():
        cols.extend(range(h * HEAD_DIM, (h + 1) * HEAD_DIM))
    return jnp.array(cols, dtype=jnp.int32)


def _block_diag(w):
    depth, blocks, n, _ = w.shape
    eye = jnp.eye(blocks, dtype=w.dtype)
    return jnp.einsum('lhij,hg->lhigj', w, eye).reshape(depth, blocks * n, blocks * n)


def kernel(x, positions, norm_ffn1, ffn1_gate, ffn1_up, ffn1_down, norm_mix, w_in, attn_sinks,
           conv_w, conv_b, rg_w_r, rg_b_r, rg_w_i, rg_b_i, rg_lambda, w_out,
           norm_ffn2, ffn2_gate, ffn2_up, ffn2_down, norm_final):
    batch, seq, _ = x.shape
    depth = w_in.shape[0]
    n = batch * seq
    assert seq % B_TILE == 0
    tm = 512

    perm = _a_column_perm()
    a_end = A_WIDTH + 2 * A_KV_WIDTH
    b_end = a_end + 3 * B_WIDTH
    w_in_b = jnp.concatenate([w_in[:, :, b_end:],
                              w_in[:, :, :A_WIDTH][:, :, perm] * SCALE,
                              w_in[:, :, A_WIDTH:a_end],
                              w_in[:, :, a_end:a_end + B_WIDTH] * SCALE,
                              w_in[:, :, a_end + B_WIDTH:b_end]], axis=2).astype(BF16)
    w_out_b = jnp.concatenate([w_out[:, :A_WIDTH][:, perm], w_out[:, A_WIDTH:]], axis=1).astype(BF16)
    sinks = jnp.repeat(attn_sinks[:, jnp.array(_a_head_order())], HEAD_DIM, axis=1)
    sinks = sinks.reshape(depth, A_SLABS, LANES)
    w_gates = jnp.concatenate([_block_diag(rg_w_r), _block_diag(rg_w_i)], axis=2).astype(BF16)
    b_gates = jnp.concatenate([rg_b_r.reshape(depth, 1, C_WIDTH), rg_b_i.reshape(depth, 1, C_WIDTH)], axis=2)
    ffn_w = [(g.astype(BF16), u.astype(BF16), d.astype(BF16))
             for g, u, d in ((ffn1_gate, ffn1_up, ffn1_down), (ffn2_gate, ffn2_up, ffn2_down))]
    vec = lambda v: v.reshape(depth, 1, -1)
    gf = norm_final.reshape(1, D_MODEL)

    cos, sin, keep = _rope_tables(positions)
    h = x.reshape(n, D_MODEL)
    for l in range(depth):
        h = _ffn(h, vec(norm_ffn1), *ffn_w[0], gf, l, final=False, tm=tm)
        qa, ka, va, qb, kb, vb, q4, k4, v4, oc = _inproj(
            h, vec(norm_mix), w_in_b, cos, sin, keep, conv_w, vec(conv_b), w_gates, b_gates,
            vec(rg_lambda), l, tm=tm, seq=seq)
        oa = _attn_a(qa, ka, va, sinks[l], batch=batch, seq=seq, tq=tm)
        ob = _attn_b(qb, kb, vb, q4, k4, v4, batch=batch, seq=seq)
        h = _ffn(h, vec(norm_ffn2), *ffn_w[1], gf, l, final=(l == depth - 1), tm=tm,
                 mix=(oa, ob, oc, w_out_b))
    return h.reshape(batch, seq, D_MODEL)
```

```python
import functools

import jax
import jax.numpy as jnp
from jax import lax
from jax.experimental import pallas as pl
from jax.experimental.pallas import tpu as pltpu

D_MODEL = 1024
HEAD_DIM = 64
A_Q_HEADS = 4
A_KV_HEADS = 2
A_WINDOW = 128
B_HEADS = 6
B_BRANCHES = ((128, 1), (512, 4), (2048, 16))
C_WIDTH = 384
C_BLOCKS = 6
C_CONV = 4
C_EXP = 8.0
D_FF = 2816
BLOCK = 128
ROPE_THETA = 10000.0
EPS = 1e-6
SCALE = HEAD_DIM ** -0.5

A_WIDTH = A_Q_HEADS * HEAD_DIM
A_KV_WIDTH = A_KV_HEADS * HEAD_DIM
B_WIDTH = B_HEADS * HEAD_DIM
MIX_WIDTH = A_WIDTH + B_WIDTH + C_WIDTH
IN_COLS = A_WIDTH + 2 * A_KV_WIDTH + 3 * B_WIDTH + 2 * C_WIDTH

LANES = 128
SUBLANES = 8
FF_CHUNK = 256
N_FF_CHUNKS = D_FF // FF_CHUNK
B_SLABS = B_WIDTH // LANES
A_SLABS = A_WIDTH // LANES
MAX_DIL = max(d for _, d in B_BRANCHES)
B_TILE = BLOCK * MAX_DIL
B_SPLIT = 4
RG_CHUNK = 64
VMEM_LIMIT = 56 * 1024 * 1024

F32 = jnp.float32
BF16 = jnp.bfloat16


def _params(sem):
    return pltpu.CompilerParams(dimension_semantics=sem, vmem_limit_bytes=VMEM_LIMIT)


def _resident(shape, index_map):
    return pl.BlockSpec(shape, index_map, pipeline_mode=pl.Buffered(1))


def _rms(x, g):
    return x * lax.rsqrt(jnp.mean(x * x, axis=-1, keepdims=True) + EPS) * g


def _tables_kernel(pos_ref, inv_ref, cos_ref, sin_ref, keep_ref):
    inv = inv_ref[...]
    reps = LANES // HEAD_DIM
    for r in range(pos_ref.shape[0]):
        p = pos_ref[r:r + 1, :].astype(F32)
        ang_t = inv * p
        cos_t, sin_t = jnp.cos(ang_t), jnp.sin(ang_t)
        rows = pl.ds(r * LANES, LANES)
        cos_ref[rows, :] = jnp.concatenate([cos_t, cos_t] * reps, axis=0).T
        sin_ref[rows, :] = jnp.concatenate([-sin_t, sin_t] * reps, axis=0).T
        keep_ref[rows, :] = jnp.where(jnp.broadcast_to(p, (LANES, LANES)) == 0.0, 0.0, 1.0).T


def _rope_tables(positions):
    n = positions.size
    rows = SUBLANES
    pos2 = positions.reshape(n // LANES, LANES)
    half = HEAD_DIM // 2
    inv = 1.0 / (ROPE_THETA ** (jnp.arange(0, HEAD_DIM, 2, dtype=F32) / HEAD_DIM))
    tab = jax.ShapeDtypeStruct((n, LANES), F32)
    out_spec = pl.BlockSpec((rows * LANES, LANES), lambda i: (i, 0))
    return pl.pallas_call(
        _tables_kernel,
        grid=(n // (rows * LANES),),
        in_specs=[pl.BlockSpec((rows, LANES), lambda i: (i, 0)),
                  pl.BlockSpec((half, 1), lambda i: (0, 0))],
        out_specs=[out_spec, out_spec, out_spec],
        out_shape=[tab, tab, tab],
        compiler_params=_params(("arbitrary",)),
        name="rope_tables",
    )(pos2, inv.reshape(half, 1))


def _ffn_kernel(*refs, mixed, final, tiles_per_seq=None):
    if mixed:
        (x_in_ref, a_ref, b_ref, pc_ref, keep_ref, cw_ref, cb_ref, wgate_ref, bgate_ref, lam_ref,
         wo_ref, *refs, x_ref, oc_ref, xbuf_ref, sa_ref, sb_ref, cin_ref) = refs
        step = pl.program_id(0)

        @pl.when(step == 0)
        def _():
            oc_ref[...] = jnp.zeros_like(oc_ref)
            cin_ref[...] = jnp.zeros_like(cin_ref)
            xbuf_ref[...] = jnp.zeros_like(xbuf_ref)

        @pl.when(step % tiles_per_seq == 0)
        def _():
            cin_ref[...] = jnp.zeros_like(cin_ref)
            xbuf_ref[0:SUBLANES, :] = jnp.zeros((SUBLANES, C_WIDTH), F32)

        mix = jnp.concatenate([a_ref[...]] + [b_ref[s] for s in range(B_SLABS)] + [oc_ref[...]], axis=1)
        x_ref[...] = x_in_ref[...] + jnp.dot(mix, wo_ref[...], preferred_element_type=F32)
        phases = _rglru_phases(pc_ref, keep_ref, oc_ref, cw_ref, cb_ref, wgate_ref, bgate_ref, lam_ref,
                               xbuf_ref, sa_ref, sb_ref, cin_ref)
        heavy, light = phases
        assert len(heavy) <= N_FF_CHUNKS - 3
        slots = [[] for _ in range(N_FF_CHUNKS)]
        for k, phase in enumerate(heavy):
            slots[1 + k].append(phase)
        tail = list(range(1 + len(heavy), N_FF_CHUNKS))
        for k, phase in enumerate(light):
            slots[tail[k * len(tail) // len(light)]].append(phase)
    else:
        x_ref, *refs = refs
        slots = [[] for _ in range(N_FF_CHUNKS)]
    g_ref, wg_ref, wu_ref, wd_ref, gf_ref, o_ref, xn_ref, acc_ref = refs
    xn_ref[...] = _rms(x_ref[...], g_ref[...]).astype(BF16)
    for c in range(N_FF_CHUNKS):
        for phase in slots[c]:
            phase()
        xn = xn_ref[...]
        cols = slice(c * FF_CHUNK, (c + 1) * FF_CHUNK)
        g = jnp.dot(xn, wg_ref[:, cols], preferred_element_type=F32)
        u = jnp.dot(xn, wu_ref[:, cols], preferred_element_type=F32)
        h = (g * (1.0 / (1.0 + jnp.exp(-g))) * u).astype(BF16)
        d = jnp.dot(h, wd_ref[cols, :], preferred_element_type=F32)
        if c == 0:
            acc_ref[...] = d
        elif c < N_FF_CHUNKS - 1:
            acc_ref[...] += d
        else:
            y = x_ref[...] + 0.5 * (acc_ref[...] + d)
            if final:
                y = _rms(y, gf_ref[...])
            o_ref[...] = y


def _ffn(x, g, wg, wu, wd, gf, layer, *, final, tm, mix=None, seq=None):
    n = x.shape[0]
    tiles = n // tm
    per_layer = lambda rows, cols: _resident((None, rows, cols), lambda i: (layer, 0, 0))
    scratch = [pltpu.VMEM((tm, D_MODEL), BF16), pltpu.VMEM((tm, D_MODEL), F32)]
    if mix is None:
        steps = tiles
        row = lambda width: pl.BlockSpec((tm, width), lambda i: (i, 0))
        in_specs = [row(D_MODEL)]
        args = [x]
    else:
        steps = tiles + 1
        row = lambda width: pl.BlockSpec((tm, width), lambda i: (jnp.maximum(i - 1, 0), 0))
        ahead = lambda width: pl.BlockSpec((tm, width), lambda i: (jnp.minimum(i, tiles - 1), 0))
        n_slabs = C_WIDTH // LANES
        in_specs = [row(D_MODEL), row(A_WIDTH),
                    pl.BlockSpec((B_SLABS, tm, LANES), lambda i: (0, jnp.maximum(i - 1, 0), 0)),
                    ahead(2 * C_WIDTH), ahead(LANES),
                    per_layer(C_CONV, C_WIDTH), per_layer(1, C_WIDTH), per_layer(C_WIDTH, 2 * C_WIDTH),
                    per_layer(1, 2 * C_WIDTH), per_layer(1, C_WIDTH), per_layer(MIX_WIDTH, D_MODEL)]
        args = [x] + list(mix)
        scratch += [pltpu.VMEM((tm, D_MODEL), F32), pltpu.VMEM((tm, C_WIDTH), BF16),
                    pltpu.VMEM((tm + SUBLANES, C_WIDTH), F32),
                    pltpu.VMEM((n_slabs, tm, LANES), F32), pltpu.VMEM((n_slabs, tm, LANES), F32),
                    pltpu.VMEM((n_slabs, SUBLANES + tm // SUBLANES, LANES), F32)]
    in_specs += [per_layer(1, D_MODEL), per_layer(D_MODEL, D_FF), per_layer(D_MODEL, D_FF),
                 per_layer(D_FF, D_MODEL), _resident((1, D_MODEL), lambda i: (0, 0))]
    args += [g, wg, wu, wd, gf]
    return pl.pallas_call(
        functools.partial(_ffn_kernel, mixed=mix is not None, final=final,
                          tiles_per_seq=None if mix is None else seq // tm),
        grid=(steps,),
        in_specs=in_specs,
        out_specs=row(D_MODEL),
        out_shape=jax.ShapeDtypeStruct((n, D_MODEL), F32),
        scratch_shapes=scratch,
        compiler_params=_params(("parallel",) if mix is None else ("arbitrary",)),
        name="ffn_mix" if mix is not None else "ffn",
    )(*args)


def _rope(v, cos, sin_signed, first_half):
    rot = jnp.where(first_half, pltpu.roll(v, LANES - HEAD_DIM // 2, axis=1),
                    pltpu.roll(v, HEAD_DIM // 2, axis=1))
    return v * cos + rot * sin_signed


def _inproj_kernel(x_ref, g_ref, w_ref, cos_ref, sin_ref,
                   qa_ref, ka_ref, va_ref, qb_ref, kb_ref, vb_ref, q4_ref, k4_ref, v4_ref, pc_ref,
                   tmp_ref):
    xn = _rms(x_ref[...], g_ref[...]).astype(BF16)
    cos = cos_ref[...]
    sin = sin_ref[...]
    first_half = lax.broadcasted_iota(jnp.int32, cos.shape, 1) % HEAD_DIM < HEAD_DIM // 2

    group = 4 * LANES
    groups = [jnp.dot(xn, w_ref[:, c:min(c + group, IN_COLS)], preferred_element_type=F32)
              for c in range(0, IN_COLS, group)]

    def proj(col, width):
        parts = [groups[c // group][:, c % group:c % group + LANES] for c in range(col, col + width, LANES)]
        return parts[0] if len(parts) == 1 else jnp.concatenate(parts, axis=1)

    pc_ref[...] = proj(0, 2 * C_WIDTH)
    col = 2 * C_WIDTH
    for s in range(A_SLABS):
        qa_ref[:, s * LANES:(s + 1) * LANES] = _rope(proj(col, LANES), cos, sin, first_half).astype(BF16)
        col += LANES
    ka_ref[...] = _rope(proj(col, LANES), cos, sin, first_half).astype(BF16)
    col += LANES
    va_ref[...] = proj(col, LANES).astype(BF16)
    col += LANES
    sub = x_ref.shape[0] // B_SPLIT
    for t, (ref, split_ref, rotary) in enumerate(((qb_ref, q4_ref, True), (kb_ref, k4_ref, True),
                                                  (vb_ref, v4_ref, False))):
        for s in range(B_SLABS):
            v = proj(col, LANES)
            if rotary:
                v = _rope(v, cos, sin, first_half)
            ref[s] = v.astype(BF16)
            tmp_ref[t * B_SLABS + s] = v
            for c in range(B_SPLIT):
                split_ref[s, c] = tmp_ref[t * B_SLABS + s, pl.ds(c, sub, stride=B_SPLIT), :]
            col += LANES


def _rglru_phases(pc_ref, keep_ref, oc_ref, cw_ref, cb_ref, wg_ref, bg_ref, lam_ref,
                  xbuf_ref, a_ref, b_ref, cin_ref):
    tm = pc_ref.shape[0]
    n_slabs = C_WIDTH // LANES
    last = SUBLANES - 1
    groups = tm // SUBLANES
    chunks = range(0, tm, RG_CHUNK)

    def gates_and_group_scan(r0):
        lam = lam_ref[...]
        softplus = jnp.maximum(-lam, 0.0) + jnp.log(1.0 + jnp.exp(-jnp.abs(lam)))
        in_group = lax.broadcasted_iota(jnp.int32, (RG_CHUNK, C_WIDTH), 0) % SUBLANES
        rows = slice(r0, r0 + RG_CHUNK)
        xc = pc_ref[rows, :C_WIDTH]
        xbuf_ref[SUBLANES + r0:SUBLANES + r0 + RG_CHUNK, :] = xc
        y = cb_ref[...] + cw_ref[0:1, :] * xc
        for j in range(1, C_CONV):
            y = y + cw_ref[j:j + 1, :] * xbuf_ref[pl.ds(SUBLANES + r0 - j, RG_CHUNK), :]

        gates = jnp.dot(y.astype(BF16), wg_ref[...], preferred_element_type=F32) + bg_ref[...]
        gates = 1.0 / (1.0 + jnp.exp(-gates))
        r, ig = gates[:, :C_WIDTH], gates[:, C_WIDTH:]
        a = jnp.exp(-C_EXP * r * softplus)
        keep = jnp.concatenate([keep_ref[rows, :]] * n_slabs, axis=1) != 0.0
        gap = 1.0 - a * a
        root = jnp.where(gap > 0.0, gap * lax.rsqrt(gap), 0.0)
        b = jnp.where(keep, root, 1.0) * (ig * y)
        a = jnp.where(keep, a, 0.0)

        sh = 1
        while sh < SUBLANES:
            ok = in_group >= sh
            b = b + a * jnp.where(ok, pltpu.roll(b, sh, axis=0), 0.0)
            a = a * jnp.where(ok, pltpu.roll(a, sh, axis=0), 1.0)
            sh *= 2
        for s in range(n_slabs):
            a_ref[s, rows, :] = a[:, s * LANES:(s + 1) * LANES]
            b_ref[s, rows, :] = b[:, s * LANES:(s + 1) * LANES]

    def group_carries():
        xbuf_ref[0:SUBLANES, :] = xbuf_ref[tm:tm + SUBLANES, :]
        grp = lax.broadcasted_iota(jnp.int32, (groups, LANES), 0)
        for s in range(n_slabs):
            a_end = a_ref[s, pl.ds(last, groups, stride=SUBLANES), :]
            b_end = b_ref[s, pl.ds(last, groups, stride=SUBLANES), :]
            b_end = b_end + jnp.where(grp == 0, a_end * cin_ref[s, last:SUBLANES, :], 0.0)
            sh = 1
            while sh < groups:
                ok = grp >= sh
                b_end = b_end + a_end * jnp.where(ok, pltpu.roll(b_end, sh, axis=0), 0.0)
                a_end = a_end * jnp.where(ok, pltpu.roll(a_end, sh, axis=0), 1.0)
                sh *= 2
            cin_ref[s, SUBLANES:, :] = b_end

    def outputs(r0):
        rows = slice(r0, r0 + RG_CHUNK)
        slabs = []
        for s in range(n_slabs):
            hs = []
            for g in range(r0 // SUBLANES, (r0 + RG_CHUNK) // SUBLANES):
                grows = slice(g * SUBLANES, (g + 1) * SUBLANES)
                hs.append(b_ref[s, grows, :] + a_ref[s, grows, :] * cin_ref[s, last + g:SUBLANES + g, :])
            slabs.append(jnp.concatenate(hs, axis=0))
        gate = pc_ref[rows, C_WIDTH:]
        gelu = 0.5 * gate * (1.0 + jnp.tanh(0.7978845608028654 * (gate + 0.044715 * gate * gate * gate)))
        oc_ref[rows, :] = (jnp.concatenate(slabs, axis=1) * gelu).astype(BF16)
        if r0 == chunks[-1]:
            for s in range(n_slabs):
                cin_ref[s, last:SUBLANES, :] = cin_ref[s, last + groups:SUBLANES + groups, :]

    return ([functools.partial(gates_and_group_scan, r0) for r0 in chunks],
            [group_carries] + [functools.partial(outputs, r0) for r0 in chunks])


def _inproj(x, g, w, cos, sin, layer, *, tm):
    n = x.shape[0]
    row = lambda width: pl.BlockSpec((tm, width), lambda i: (i, 0))
    slab = pl.BlockSpec((B_SLABS, tm, LANES), lambda i: (0, i, 0))
    per_tile = B_TILE // tm
    split = pl.BlockSpec((B_SLABS, None, B_SPLIT, tm // B_SPLIT, LANES),
                         lambda i: (0, i // per_tile, 0, i % per_tile, 0))
    sds = jax.ShapeDtypeStruct
    nat = sds((B_SLABS, n, LANES), BF16)
    cls = sds((B_SLABS, n // B_TILE, B_SPLIT, B_TILE // B_SPLIT, LANES), F32)
    per_layer = lambda rows, cols: _resident((None, rows, cols), lambda i: (layer, 0, 0))
    return pl.pallas_call(
        _inproj_kernel,
        grid=(n // tm,),
        in_specs=[row(D_MODEL), per_layer(1, D_MODEL), per_layer(D_MODEL, IN_COLS),
                  row(LANES), row(LANES)],
        out_specs=[row(A_WIDTH), row(A_KV_WIDTH), row(A_KV_WIDTH), slab, slab, slab,
                   split, split, split, row(2 * C_WIDTH)],
        out_shape=[sds((n, A_WIDTH), BF16), sds((n, A_KV_WIDTH), BF16), sds((n, A_KV_WIDTH), BF16),
                   nat, nat, nat, cls, cls, cls, sds((n, 2 * C_WIDTH), F32)],
        scratch_shapes=[pltpu.VMEM((3 * B_SLABS, tm, LANES), F32)],
        compiler_params=_params(("parallel",)),
        name="inproj",
    )(x, g, w, cos, sin)


def _band_bias(max_dist):
    row = lax.broadcasted_iota(jnp.int32, (2 * BLOCK, 2 * BLOCK), 0) % BLOCK
    col = lax.broadcasted_iota(jnp.int32, (2 * BLOCK, 2 * BLOCK), 1)
    first_col = row + (BLOCK - max_dist)
    out = []
    for lo in (first_col, jnp.maximum(first_col, BLOCK)):
        out.append(jnp.where((col >= lo) & (col <= row + BLOCK), 0.0, -jnp.inf).astype(F32))
    return jnp.stack(out)


def _attn_unit(q, kcat, vcat, bias):
    lane = lax.broadcasted_iota(jnp.int32, (BLOCK, LANES), 1)
    left = lane < HEAD_DIM
    zero = jnp.zeros_like(q)
    q2 = jnp.concatenate([jnp.where(left, q, zero), jnp.where(left, zero, q)], axis=0)
    s = lax.dot_general(q2, kcat, (((1,), (1,)), ((), ())), preferred_element_type=F32) + bias
    m = jnp.max(s, axis=-1, keepdims=True)
    p = jnp.exp(s - m).astype(BF16)
    v1 = jnp.concatenate([vcat, jnp.ones_like(vcat)], axis=1)
    r = jnp.dot(p, v1, preferred_element_type=F32)
    num = jnp.where(left, r[:BLOCK, :LANES], r[BLOCK:, :LANES])
    den = jnp.where(left, r[:BLOCK, LANES:], r[BLOCK:, LANES:])
    mx = jnp.where(left, jnp.broadcast_to(m[:BLOCK], (BLOCK, LANES)),
                   jnp.broadcast_to(m[BLOCK:], (BLOCK, LANES)))
    return num, den, mx


def _attn_a_kernel(q_ref, kc_ref, kp_ref, vc_ref, vp_ref, sink_ref, bias_ref, o_ref, *, tq):
    no_prev = jnp.where(pl.program_id(1) == 0, 1, 0)
    for j in range(tq // BLOCK):
        rows = slice(j * BLOCK, (j + 1) * BLOCK)
        if j == 0:
            kprev, vprev = kp_ref[...], vp_ref[...]
            bias = bias_ref[no_prev]
        else:
            prev_rows = slice((j - 1) * BLOCK, j * BLOCK)
            kprev, vprev = kc_ref[prev_rows, :], vc_ref[prev_rows, :]
            bias = bias_ref[0]
        kcat = jnp.concatenate([kprev, kc_ref[rows, :]], axis=0)
        vcat = jnp.concatenate([vprev, vc_ref[rows, :]], axis=0)
        for s in range(A_SLABS):
            lanes = slice(s * LANES, (s + 1) * LANES)
            num, den, mx = _attn_unit(q_ref[rows, lanes], kcat, vcat, bias)
            o_ref[rows, lanes] = (num / (den + jnp.exp(sink_ref[s:s + 1, :] - mx))).astype(BF16)


def _attn_a(qa, ka, va, sinks, *, batch, seq, tq):
    tiles = seq // tq
    ratio = tq // BLOCK
    cur = lambda width: pl.BlockSpec((tq, width), lambda b, i: (b * tiles + i, 0))
    prev = pl.BlockSpec((BLOCK, A_KV_WIDTH),
                        lambda b, i: (jnp.maximum((b * tiles + i) * ratio - 1, 0), 0))
    return pl.pallas_call(
        functools.partial(_attn_a_kernel, tq=tq),
        grid=(batch, tiles),
        in_specs=[cur(A_WIDTH), cur(A_KV_WIDTH), prev, cur(A_KV_WIDTH), prev,
                  _resident((A_SLABS, LANES), lambda b, i: (0, 0)),
                  _resident((2, 2 * BLOCK, 2 * BLOCK), lambda b, i: (0, 0, 0))],
        out_specs=cur(A_WIDTH),
        out_shape=jax.ShapeDtypeStruct((batch * seq, A_WIDTH), BF16),
        compiler_params=_params(("parallel", "parallel")),
        name="attn_a",
    )(qa, ka, ka, va, va, sinks, _band_bias(A_WINDOW - 1))


def _attn_b_kernel(qn_ref, kn_ref, knp_ref, vn_ref, vnp_ref, q4_ref, k4_ref, k4p_ref, v4_ref, v4p_ref,
                   bias_ref, o_ref, num_ref, den_ref, mx_ref, cls_ref):
    no_prev = jnp.where(pl.program_id(1) == 0, 1, 0)
    nbr = len(B_BRANCHES)
    sub = B_TILE // B_SPLIT
    results = (num_ref, den_ref, mx_ref)

    def run(q, kprev, kcur, vprev, vcur, bias):
        return _attn_unit(q.astype(BF16), jnp.concatenate([kprev, kcur], axis=0).astype(BF16),
                          jnp.concatenate([vprev, vcur], axis=0).astype(BF16), bias)

    def merge(rows, first):
        parts = [first] + [tuple(ref[br, rows, :] for ref in results) for br in strided]
        top = functools.reduce(jnp.maximum, [p[2] for p in parts])
        es = [jnp.exp(p[2] - top) for p in parts]
        num = functools.reduce(jnp.add, [p[0] * e for p, e in zip(parts, es)])
        den = functools.reduce(jnp.add, [p[1] * e for p, e in zip(parts, es)])
        o_ref[rows, :] = (num / den).astype(BF16)

    order = sorted(range(nbr), key=lambda br: B_BRANCHES[br][1] == 1)
    strided = [br for br in order if B_BRANCHES[br][1] != 1]
    for br in order:
        window, d = B_BRANCHES[br]
        assert window // d == BLOCK
        if d == 1:
            kprev, vprev = knp_ref[...], vnp_ref[...]
            for j in range(B_TILE // BLOCK):
                rows = slice(j * BLOCK, (j + 1) * BLOCK)
                kcur, vcur = kn_ref[rows, :], vn_ref[rows, :]
                merge(rows, run(qn_ref[rows, :], kprev, kcur, vprev, vcur,
                                bias_ref[no_prev if j == 0 else 0]))
                kprev, vprev = kcur, vcur
        elif d == B_SPLIT:
            for c in range(B_SPLIT):
                kprev, vprev = k4p_ref[c, sub - BLOCK:sub, :], v4p_ref[c, sub - BLOCK:sub, :]
                for j in range(sub // BLOCK):
                    rows = slice(j * BLOCK, (j + 1) * BLOCK)
                    kcur, vcur = k4_ref[c, rows, :], v4_ref[c, rows, :]
                    res = run(q4_ref[c, rows, :], kprev, kcur, vprev, vcur,
                              bias_ref[no_prev if j == 0 else 0])
                    for ref, val in zip(results, res):
                        ref[br, pl.ds(j * BLOCK * d + c, BLOCK, stride=d), :] = val
                    kprev, vprev = kcur, vcur
        else:
            inner = d // B_SPLIT
            assert d % B_SPLIT == 0 and sub == BLOCK * inner
            for c in range(B_SPLIT):
                for ri in range(inner):
                    rows = pl.ds(ri, BLOCK, stride=inner)
                    res = run(q4_ref[c, rows, :], k4p_ref[c, rows, :], k4_ref[c, rows, :],
                              v4p_ref[c, rows, :], v4_ref[c, rows, :], bias_ref[no_prev])
                    for a, val in enumerate(res):
                        cls_ref[a, c, rows, :] = val
            for a, ref in enumerate(results):
                for c in range(B_SPLIT):
                    ref[br, pl.ds(c, sub, stride=B_SPLIT), :] = cls_ref[a, c]


def _attn_b(qn, kn, vn, q4, k4, v4, *, batch, seq):
    tiles = seq // B_TILE
    sub = B_TILE // B_SPLIT
    tile = lambda b, i: b * tiles + i
    nat = pl.BlockSpec((None, B_TILE, LANES), lambda b, i, s: (s, tile(b, i), 0))
    halo = pl.BlockSpec((None, BLOCK, LANES),
                        lambda b, i, s: (s, jnp.maximum(tile(b, i) * (B_TILE // BLOCK) - 1, 0), 0))
    cls = pl.BlockSpec((None, None, B_SPLIT, sub, LANES), lambda b, i, s: (s, tile(b, i), 0, 0, 0))
    cls_prev = pl.BlockSpec((None, None, B_SPLIT, sub, LANES),
                            lambda b, i, s: (s, jnp.maximum(tile(b, i) - 1, 0), 0, 0, 0))
    nbr = len(B_BRANCHES)
    return pl.pallas_call(
        _attn_b_kernel,
        grid=(batch, tiles, B_SLABS),
        in_specs=[nat, nat, halo, nat, halo, cls, cls, cls_prev, cls, cls_prev,
                  _resident((2, 2 * BLOCK, 2 * BLOCK), lambda b, i, s: (0, 0, 0))],
        out_specs=nat,
        out_shape=jax.ShapeDtypeStruct((B_SLABS, batch * seq, LANES), BF16),
        scratch_shapes=[pltpu.VMEM((nbr, B_TILE, LANES), F32)] * 3
                       + [pltpu.VMEM((3, B_SPLIT, sub, LANES), F32)],
        compiler_params=_params(("parallel", "parallel", "parallel")),
        name="attn_b",
    )(qn, kn, kn, vn, vn, q4, k4, k4, v4, v4, _band_bias(BLOCK))


def _a_head_order():
    group = A_Q_HEADS // A_KV_HEADS
    return [kv * group + m for m in range(group) for kv in range(A_KV_HEADS)]


def _a_column_perm():
    cols = []
    for h in _a_head_order():
        cols.extend(range(h * HEAD_DIM, (h + 1) * HEAD_DIM))
    return jnp.array(cols, dtype=jnp.int32)


def _block_diag(w):
    depth, blocks, n, _ = w.shape
    eye = jnp.eye(blocks, dtype=w.dtype)
    return jnp.einsum('lhij,hg->lhigj', w, eye).reshape(depth, blocks * n, blocks * n)


def kernel(x, positions, norm_ffn1, ffn1_gate, ffn1_up, ffn1_down, norm_mix, w_in, attn_sinks,
           conv_w, conv_b, rg_w_r, rg_b_r, rg_w_i, rg_b_i, rg_lambda, w_out,
           norm_ffn2, ffn2_gate, ffn2_up, ffn2_down, norm_final):
    batch, seq, _ = x.shape
    depth = w_in.shape[0]
    n = batch * seq
    assert seq % B_TILE == 0
    tm = 512

    perm = _a_column_perm()
    a_end = A_WIDTH + 2 * A_KV_WIDTH
    b_end = a_end + 3 * B_WIDTH
    w_in_b = jnp.concatenate([w_in[:, :, b_end:],
                              w_in[:, :, :A_WIDTH][:, :, perm] * SCALE,
                              w_in[:, :, A_WIDTH:a_end],
                              w_in[:, :, a_end:a_end + B_WIDTH] * SCALE,
                              w_in[:, :, a_end + B_WIDTH:b_end]], axis=2).astype(BF16)
    w_out_b = jnp.concatenate([w_out[:, :A_WIDTH][:, perm], w_out[:, A_WIDTH:]], axis=1).astype(BF16)
    sinks = jnp.repeat(attn_sinks[:, jnp.array(_a_head_order())], HEAD_DIM, axis=1)
    sinks = sinks.reshape(depth, A_SLABS, LANES)
    w_gates = jnp.concatenate([_block_diag(rg_w_r), _block_diag(rg_w_i)], axis=2).astype(BF16)
    b_gates = jnp.concatenate([rg_b_r.reshape(depth, 1, C_WIDTH), rg_b_i.reshape(depth, 1, C_WIDTH)], axis=2)
    ffn_w = [(g.astype(BF16), u.astype(BF16), d.astype(BF16))
             for g, u, d in ((ffn1_gate, ffn1_up, ffn1_down), (ffn2_gate, ffn2_up, ffn2_down))]
    vec = lambda v: v.reshape(depth, 1, -1)
    gf = norm_final.reshape(1, D_MODEL)

    cos, sin, keep = _rope_tables(positions)
    h = x.reshape(n, D_MODEL)
    for l in range(depth):
        h = _ffn(h, vec(norm_ffn1), *ffn_w[0], gf, l, final=False, tm=tm)
        qa, ka, va, qb, kb, vb, q4, k4, v4, pc = _inproj(h, vec(norm_mix), w_in_b, cos, sin, l, tm=tm)
        oa = _attn_a(qa, ka, va, sinks[l], batch=batch, seq=seq, tq=tm)
        ob = _attn_b(qb, kb, vb, q4, k4, v4, batch=batch, seq=seq)
        h = _ffn(h, vec(norm_ffn2), *ffn_w[1], gf, l, final=(l == depth - 1), tm=tm, seq=seq,
                 mix=(oa, ob, pc, keep, conv_w, vec(conv_b), w_gates, b_gates, vec(rg_lambda), w_out_b))
    return h.reshape(batch, seq, D_MODEL)
```

```python
import functools

import jax
import jax.numpy as jnp
from jax import lax
from jax.experimental import pallas as pl
from jax.experimental.pallas import tpu as pltpu

D_MODEL = 1024
HEAD_DIM = 64
A_Q_HEADS = 4
A_KV_HEADS = 2
A_WINDOW = 128
B_HEADS = 6
B_BRANCHES = ((128, 1), (512, 4), (2048, 16))
C_WIDTH = 384
C_BLOCKS = 6
C_CONV = 4
C_EXP = 8.0
D_FF = 2816
BLOCK = 128
ROPE_THETA = 10000.0
EPS = 1e-6
SCALE = HEAD_DIM ** -0.5
LOG2E = 1.4426950408889634

A_WIDTH = A_Q_HEADS * HEAD_DIM
A_KV_WIDTH = A_KV_HEADS * HEAD_DIM
B_WIDTH = B_HEADS * HEAD_DIM
MIX_WIDTH = A_WIDTH + B_WIDTH + C_WIDTH
IN_COLS = A_WIDTH + 2 * A_KV_WIDTH + 3 * B_WIDTH + 2 * C_WIDTH

LANES = 128
SUBLANES = 8
FF_CHUNK = 256
N_FF_CHUNKS = D_FF // FF_CHUNK
B_SLABS = B_WIDTH // LANES
A_SLABS = A_WIDTH // LANES
MAX_DIL = max(d for _, d in B_BRANCHES)
B_TILE = BLOCK * MAX_DIL
B_SPLIT = 4
RG_CHUNK = 64
VMEM_LIMIT = 56 * 1024 * 1024

F32 = jnp.float32
BF16 = jnp.bfloat16


def _params(sem):
    return pltpu.CompilerParams(dimension_semantics=sem, vmem_limit_bytes=VMEM_LIMIT)


def _resident(shape, index_map):
    return pl.BlockSpec(shape, index_map, pipeline_mode=pl.Buffered(1))


def _rms(x, g):
    return x * lax.rsqrt(jnp.mean(x * x, axis=-1, keepdims=True) + EPS) * g


def _tables_kernel(pos_ref, inv_ref, cos_ref, sin_ref, keep_ref):
    inv = inv_ref[...]
    reps = LANES // HEAD_DIM
    for r in range(pos_ref.shape[0]):
        p = pos_ref[r:r + 1, :].astype(F32)
        ang_t = inv * p
        cos_t, sin_t = jnp.cos(ang_t), jnp.sin(ang_t)
        rows = pl.ds(r * LANES, LANES)
        cos_ref[rows, :] = jnp.concatenate([cos_t, cos_t] * reps, axis=0).T
        sin_ref[rows, :] = jnp.concatenate([-sin_t, sin_t] * reps, axis=0).T
        keep_ref[rows, :] = jnp.where(jnp.broadcast_to(p, (LANES, LANES)) == 0.0, 0.0, 1.0).T


def _rope_tables(positions):
    n = positions.size
    rows = SUBLANES
    pos2 = positions.reshape(n // LANES, LANES)
    half = HEAD_DIM // 2
    inv = 1.0 / (ROPE_THETA ** (jnp.arange(0, HEAD_DIM, 2, dtype=F32) / HEAD_DIM))
    tab = jax.ShapeDtypeStruct((n, LANES), F32)
    out_spec = pl.BlockSpec((rows * LANES, LANES), lambda i: (i, 0))
    return pl.pallas_call(
        _tables_kernel,
        grid=(n // (rows * LANES),),
        in_specs=[pl.BlockSpec((rows, LANES), lambda i: (i, 0)),
                  pl.BlockSpec((half, 1), lambda i: (0, 0))],
        out_specs=[out_spec, out_spec, out_spec],
        out_shape=[tab, tab, tab],
        compiler_params=_params(("arbitrary",)),
        name="rope_tables",
    )(pos2, inv.reshape(half, 1))


def _ffn_kernel(*refs, mixed, final):
    if mixed:
        x_in_ref, a_ref, b_ref, c_ref, wo_ref, *refs, x_ref = refs
        mix = jnp.concatenate([a_ref[...]] + [b_ref[s] for s in range(B_SLABS)] + [c_ref[...]], axis=1)
        x_ref[...] = x_in_ref[...] + jnp.dot(mix, wo_ref[...], preferred_element_type=F32)
    else:
        x_ref, *refs = refs
    g_ref, wg_ref, wu_ref, wd_ref, gf_ref, o_ref, xn_ref, acc_ref = refs
    xn_ref[...] = _rms(x_ref[...], g_ref[...]).astype(BF16)
    for c in range(N_FF_CHUNKS):
        xn = xn_ref[...]
        cols = slice(c * FF_CHUNK, (c + 1) * FF_CHUNK)
        g = jnp.dot(xn, wg_ref[:, cols], preferred_element_type=F32)
        u = jnp.dot(xn, wu_ref[:, cols], preferred_element_type=F32)
        h = (g * (1.0 / (1.0 + jnp.exp(-g))) * u).astype(BF16)
        d = jnp.dot(h, wd_ref[cols, :], preferred_element_type=F32)
        if c == 0:
            acc_ref[...] = d
        elif c < N_FF_CHUNKS - 1:
            acc_ref[...] += d
        else:
            y = x_ref[...] + 0.5 * (acc_ref[...] + d)
            if final:
                y = _rms(y, gf_ref[...])
            o_ref[...] = y


def _ffn(x, g, wg, wu, wd, gf, layer, *, final, tm, mix=None):
    n = x.shape[0]
    row = lambda width: pl.BlockSpec((tm, width), lambda i: (i, 0))
    per_layer = lambda rows, cols: _resident((None, rows, cols), lambda i: (layer, 0, 0))
    in_specs = [row(D_MODEL)]
    args = [x]
    scratch = [pltpu.VMEM((tm, D_MODEL), BF16), pltpu.VMEM((tm, D_MODEL), F32)]
    if mix is not None:
        in_specs += [row(A_WIDTH), pl.BlockSpec((B_SLABS, tm, LANES), lambda i: (0, i, 0)),
                     row(C_WIDTH), per_layer(MIX_WIDTH, D_MODEL)]
        args += list(mix)
        scratch.append(pltpu.VMEM((tm, D_MODEL), F32))
    in_specs += [per_layer(1, D_MODEL), per_layer(D_MODEL, D_FF), per_layer(D_MODEL, D_FF),
                 per_layer(D_FF, D_MODEL), _resident((1, D_MODEL), lambda i: (0, 0))]
    args += [g, wg, wu, wd, gf]
    return pl.pallas_call(
        functools.partial(_ffn_kernel, mixed=mix is not None, final=final),
        grid=(n // tm,),
        in_specs=in_specs,
        out_specs=row(D_MODEL),
        out_shape=jax.ShapeDtypeStruct((n, D_MODEL), F32),
        scratch_shapes=scratch,
        compiler_params=_params(("parallel",)),
        name="ffn_mix" if mix is not None else "ffn",
    )(*args)


def _rope(v, cos, sin_signed, first_half):
    rot = jnp.where(first_half, pltpu.roll(v, LANES - HEAD_DIM // 2, axis=1),
                    pltpu.roll(v, HEAD_DIM // 2, axis=1))
    return v * cos + rot * sin_signed


def _inproj_kernel(x_ref, g_ref, w_ref, cos_ref, sin_ref, keep_ref, cw_ref, cb_ref, wgate_ref,
                   bgate_ref, lam_ref,
                   qa_ref, ka_ref, va_ref, qb_ref, kb_ref, vb_ref, q4_ref, k4_ref, v4_ref, oc_ref,
                   tmp_ref, xbuf_ref, a_ref, b_ref, cin_ref, pc_ref, *, tiles_per_seq):
    step = pl.program_id(0)

    @pl.when(step == 0)
    def _():
        pc_ref[...] = jnp.zeros_like(pc_ref)
        cin_ref[...] = jnp.zeros_like(cin_ref)
        xbuf_ref[...] = jnp.zeros_like(xbuf_ref)

    @pl.when((step + tiles_per_seq - 1) % tiles_per_seq == 0)
    def _():
        cin_ref[...] = jnp.zeros_like(cin_ref)
        xbuf_ref[0:SUBLANES, :] = jnp.zeros((SUBLANES, C_WIDTH), F32)

    heavy, light = _rglru_phases(pc_ref, keep_ref, oc_ref, cw_ref, cb_ref, wgate_ref, bgate_ref,
                                 lam_ref, xbuf_ref, a_ref, b_ref, cin_ref)

    xn = _rms(x_ref[...], g_ref[...]).astype(BF16)
    cos = cos_ref[...]
    sin = sin_ref[...]
    first_half = lax.broadcasted_iota(jnp.int32, cos.shape, 1) % HEAD_DIM < HEAD_DIM // 2
    sub = x_ref.shape[0] // B_SPLIT

    def to_a(ref, s, rotary):
        def put(v):
            v = _rope(v, cos, sin, first_half) if rotary else v
            if ref.shape[-1] == LANES:
                ref[...] = v.astype(BF16)
            else:
                ref[:, s * LANES:(s + 1) * LANES] = v.astype(BF16)
        return put

    def to_b(t, ref, split_ref, s, rotary):
        def put(v):
            v = _rope(v, cos, sin, first_half) if rotary else v
            ref[s] = v.astype(BF16)
            tmp_ref[t * B_SLABS + s] = v
            for c in range(B_SPLIT):
                split_ref[s, c] = tmp_ref[t * B_SLABS + s, pl.ds(c, sub, stride=B_SPLIT), :]
        return put

    held = []
    slab_sinks = ([to_a(qa_ref, s, True) for s in range(A_SLABS)] + [to_a(ka_ref, 0, True), to_a(va_ref, 0, False)]
                  + [to_b(0, qb_ref, q4_ref, s, True) for s in range(B_SLABS)]
                  + [to_b(1, kb_ref, k4_ref, s, True) for s in range(B_SLABS)]
                  + [to_b(2, vb_ref, v4_ref, s, False) for s in range(B_SLABS)]
                  + [held.append] * (2 * C_WIDTH // LANES))
    assert len(slab_sinks) * LANES == IN_COLS

    group = 4
    n_groups = -(-len(slab_sinks) // group)
    half = len(heavy) // 2
    fills = [heavy[:half], heavy[half:], light[:len(light) // 2 + 1], light[len(light) // 2 + 1:]]
    assert n_groups > len(fills)
    for gi in range(n_groups):
        sinks = slab_sinks[gi * group:(gi + 1) * group]
        lo = gi * group * LANES
        res = jnp.dot(xn, w_ref[:, lo:lo + len(sinks) * LANES], preferred_element_type=F32)
        for k, sink in enumerate(sinks):
            sink(res[:, k * LANES:(k + 1) * LANES])
        for phase in (fills[gi] if gi < len(fills) else []):
            phase()
    pc_ref[...] = jnp.concatenate(held, axis=1)


def _rglru_phases(pc_ref, keep_ref, oc_ref, cw_ref, cb_ref, wg_ref, bg_ref, lam_ref,
                  xbuf_ref, a_ref, b_ref, cin_ref):
    tm = pc_ref.shape[0]
    n_slabs = C_WIDTH // LANES
    last = SUBLANES - 1
    groups = tm // SUBLANES
    chunks = range(0, tm, RG_CHUNK)

    def gates_and_group_scan(r0):
        lam = lam_ref[...]
        softplus = jnp.maximum(-lam, 0.0) + jnp.log(1.0 + jnp.exp(-jnp.abs(lam)))
        in_group = lax.broadcasted_iota(jnp.int32, (RG_CHUNK, C_WIDTH), 0) % SUBLANES
        rows = slice(r0, r0 + RG_CHUNK)
        xc = pc_ref[rows, :C_WIDTH]
        xbuf_ref[SUBLANES + r0:SUBLANES + r0 + RG_CHUNK, :] = xc
        y = cb_ref[...] + cw_ref[0:1, :] * xc
        for j in range(1, C_CONV):
            y = y + cw_ref[j:j + 1, :] * xbuf_ref[pl.ds(SUBLANES + r0 - j, RG_CHUNK), :]

        gates = jnp.dot(y.astype(BF16), wg_ref[...], preferred_element_type=F32) + bg_ref[...]
        gates = 1.0 / (1.0 + jnp.exp(-gates))
        r, ig = gates[:, :C_WIDTH], gates[:, C_WIDTH:]
        a = jnp.exp(-C_EXP * r * softplus)
        keep = jnp.concatenate([keep_ref[rows, :]] * n_slabs, axis=1) != 0.0
        gap = 1.0 - a * a
        root = jnp.where(gap > 0.0, gap * lax.rsqrt(gap), 0.0)
        b = jnp.where(keep, root, 1.0) * (ig * y)
        a = jnp.where(keep, a, 0.0)

        sh = 1
        while sh < SUBLANES:
            ok = in_group >= sh
            b = b + a * jnp.where(ok, pltpu.roll(b, sh, axis=0), 0.0)
            a = a * jnp.where(ok, pltpu.roll(a, sh, axis=0), 1.0)
            sh *= 2
        for s in range(n_slabs):
            a_ref[s, rows, :] = a[:, s * LANES:(s + 1) * LANES]
            b_ref[s, rows, :] = b[:, s * LANES:(s + 1) * LANES]

    def group_carries():
        xbuf_ref[0:SUBLANES, :] = xbuf_ref[tm:tm + SUBLANES, :]
        grp = lax.broadcasted_iota(jnp.int32, (groups, LANES), 0)
        for s in range(n_slabs):
            a_end = a_ref[s, pl.ds(last, groups, stride=SUBLANES), :]
            b_end = b_ref[s, pl.ds(last, groups, stride=SUBLANES), :]
            b_end = b_end + jnp.where(grp == 0, a_end * cin_ref[s, last:SUBLANES, :], 0.0)
            sh = 1
            while sh < groups:
                ok = grp >= sh
                b_end = b_end + a_end * jnp.where(ok, pltpu.roll(b_end, sh, axis=0), 0.0)
                a_end = a_end * jnp.where(ok, pltpu.roll(a_end, sh, axis=0), 1.0)
                sh *= 2
            cin_ref[s, SUBLANES:, :] = b_end

    def outputs(r0):
        rows = slice(r0, r0 + RG_CHUNK)
        slabs = []
        for s in range(n_slabs):
            hs = []
            for g in range(r0 // SUBLANES, (r0 + RG_CHUNK) // SUBLANES):
                grows = slice(g * SUBLANES, (g + 1) * SUBLANES)
                hs.append(b_ref[s, grows, :] + a_ref[s, grows, :] * cin_ref[s, last + g:SUBLANES + g, :])
            slabs.append(jnp.concatenate(hs, axis=0))
        gate = pc_ref[rows, C_WIDTH:]
        gelu = 0.5 * gate * (1.0 + jnp.tanh(0.7978845608028654 * (gate + 0.044715 * gate * gate * gate)))
        oc_ref[rows, :] = (jnp.concatenate(slabs, axis=1) * gelu).astype(BF16)
        if r0 == chunks[-1]:
            for s in range(n_slabs):
                cin_ref[s, last:SUBLANES, :] = cin_ref[s, last + groups:SUBLANES + groups, :]

    return ([functools.partial(gates_and_group_scan, r0) for r0 in chunks],
            [group_carries] + [functools.partial(outputs, r0) for r0 in chunks])


def _inproj(x, g, w, cos, sin, keep, cw, cb, wgate, bgate, lam, layer, *, tm, seq):
    n = x.shape[0]
    tiles = n // tm
    n_slabs = C_WIDTH // LANES
    cur = lambda i: jnp.minimum(i, tiles - 1)
    lag = lambda i: jnp.maximum(i - 1, 0)
    row = lambda width: pl.BlockSpec((tm, width), lambda i: (cur(i), 0))
    lag_row = lambda width: pl.BlockSpec((tm, width), lambda i: (lag(i), 0))
    slab = pl.BlockSpec((B_SLABS, tm, LANES), lambda i: (0, cur(i), 0))
    per_tile = B_TILE // tm
    split = pl.BlockSpec((B_SLABS, None, B_SPLIT, tm // B_SPLIT, LANES),
                         lambda i: (0, cur(i) // per_tile, 0, cur(i) % per_tile, 0))
    sds = jax.ShapeDtypeStruct
    nat = sds((B_SLABS, n, LANES), BF16)
    cls = sds((B_SLABS, n // B_TILE, B_SPLIT, B_TILE // B_SPLIT, LANES), F32)
    per_layer = lambda rows, cols: _resident((None, rows, cols), lambda i: (layer, 0, 0))
    return pl.pallas_call(
        functools.partial(_inproj_kernel, tiles_per_seq=seq // tm),
        grid=(tiles + 1,),
        in_specs=[row(D_MODEL), per_layer(1, D_MODEL), per_layer(D_MODEL, IN_COLS),
                  row(LANES), row(LANES), lag_row(LANES),
                  per_layer(C_CONV, C_WIDTH), per_layer(1, C_WIDTH),
                  per_layer(C_WIDTH, 2 * C_WIDTH), per_layer(1, 2 * C_WIDTH), per_layer(1, C_WIDTH)],
        out_specs=[row(A_WIDTH), row(A_KV_WIDTH), row(A_KV_WIDTH), slab, slab, slab,
                   split, split, split, lag_row(C_WIDTH)],
        out_shape=[sds((n, A_WIDTH), BF16), sds((n, A_KV_WIDTH), BF16), sds((n, A_KV_WIDTH), BF16),
                   nat, nat, nat, cls, cls, cls, sds((n, C_WIDTH), BF16)],
        scratch_shapes=[pltpu.VMEM((3 * B_SLABS, tm, LANES), F32),
                        pltpu.VMEM((tm + SUBLANES, C_WIDTH), F32),
                        pltpu.VMEM((n_slabs, tm, LANES), F32), pltpu.VMEM((n_slabs, tm, LANES), F32),
                        pltpu.VMEM((n_slabs, SUBLANES + tm // SUBLANES, LANES), F32),
                        pltpu.VMEM((tm, 2 * C_WIDTH), F32)],
        compiler_params=_params(("arbitrary",)),
        name="inproj",
    )(x, g, w, cos, sin, keep, cw, cb, wgate, bgate, lam)


def _band_bias(max_dist):
    row = lax.broadcasted_iota(jnp.int32, (2 * BLOCK, 2 * BLOCK), 0) % BLOCK
    col = lax.broadcasted_iota(jnp.int32, (2 * BLOCK, 2 * BLOCK), 1)
    first_col = row + (BLOCK - max_dist)
    out = []
    for lo in (first_col, jnp.maximum(first_col, BLOCK)):
        out.append(jnp.where((col >= lo) & (col <= row + BLOCK), 0.0, -jnp.inf).astype(F32))
    return jnp.stack(out)


def _attn_unit(q, kcat, vcat, bias):
    lane = lax.broadcasted_iota(jnp.int32, (BLOCK, LANES), 1)
    left = lane < HEAD_DIM
    zero = jnp.zeros_like(q)
    q2 = jnp.concatenate([jnp.where(left, q, zero), jnp.where(left, zero, q)], axis=0)
    s = lax.dot_general(q2, kcat, (((1,), (1,)), ((), ())), preferred_element_type=F32) + bias
    m = jnp.max(s, axis=-1, keepdims=True)
    p = jnp.exp2(s - m).astype(BF16)
    v1 = jnp.concatenate([vcat, jnp.ones_like(vcat)], axis=1)
    r = jnp.dot(p, v1, preferred_element_type=F32)
    num = jnp.where(left, r[:BLOCK, :LANES], r[BLOCK:, :LANES])
    den = jnp.where(left, r[:BLOCK, LANES:], r[BLOCK:, LANES:])
    mx = jnp.where(left, jnp.broadcast_to(m[:BLOCK], (BLOCK, LANES)),
                   jnp.broadcast_to(m[BLOCK:], (BLOCK, LANES)))
    return num, den, mx


def _attn_a_kernel(q_ref, kc_ref, kp_ref, vc_ref, vp_ref, sink_ref, bias_ref, o_ref, *, tq):
    no_prev = jnp.where(pl.program_id(1) == 0, 1, 0)
    for j in range(tq // BLOCK):
        rows = slice(j * BLOCK, (j + 1) * BLOCK)
        if j == 0:
            kprev, vprev = kp_ref[...], vp_ref[...]
            bias = bias_ref[no_prev]
        else:
            prev_rows = slice((j - 1) * BLOCK, j * BLOCK)
            kprev, vprev = kc_ref[prev_rows, :], vc_ref[prev_rows, :]
            bias = bias_ref[0]
        kcat = jnp.concatenate([kprev, kc_ref[rows, :]], axis=0)
        vcat = jnp.concatenate([vprev, vc_ref[rows, :]], axis=0)
        for s in range(A_SLABS):
            lanes = slice(s * LANES, (s + 1) * LANES)
            num, den, mx = _attn_unit(q_ref[rows, lanes], kcat, vcat, bias)
            o_ref[rows, lanes] = (num / (den + jnp.exp2(sink_ref[s:s + 1, :] - mx))).astype(BF16)


def _attn_a(qa, ka, va, sinks, *, batch, seq, tq):
    tiles = seq // tq
    ratio = tq // BLOCK
    cur = lambda width: pl.BlockSpec((tq, width), lambda b, i: (b * tiles + i, 0))
    prev = pl.BlockSpec((BLOCK, A_KV_WIDTH),
                        lambda b, i: (jnp.maximum((b * tiles + i) * ratio - 1, 0), 0))
    return pl.pallas_call(
        functools.partial(_attn_a_kernel, tq=tq),
        grid=(batch, tiles),
        in_specs=[cur(A_WIDTH), cur(A_KV_WIDTH), prev, cur(A_KV_WIDTH), prev,
                  _resident((A_SLABS, LANES), lambda b, i: (0, 0)),
                  _resident((2, 2 * BLOCK, 2 * BLOCK), lambda b, i: (0, 0, 0))],
        out_specs=cur(A_WIDTH),
        out_shape=jax.ShapeDtypeStruct((batch * seq, A_WIDTH), BF16),
        compiler_params=_params(("parallel", "parallel")),
        name="attn_a",
    )(qa, ka, ka, va, va, sinks, _band_bias(A_WINDOW - 1))


def _attn_b_kernel(qn_ref, kn_ref, knp_ref, vn_ref, vnp_ref, q4_ref, k4_ref, k4p_ref, v4_ref, v4p_ref,
                   bias_ref, o_ref, num_ref, den_ref, mx_ref, cls_ref):
    no_prev = jnp.where(pl.program_id(1) == 0, 1, 0)
    nbr = len(B_BRANCHES)
    sub = B_TILE // B_SPLIT
    results = (num_ref, den_ref, mx_ref)

    def run(q, kprev, kcur, vprev, vcur, bias):
        return _attn_unit(q.astype(BF16), jnp.concatenate([kprev, kcur], axis=0).astype(BF16),
                          jnp.concatenate([vprev, vcur], axis=0).astype(BF16), bias)

    def merge(rows, first):
        parts = [first] + [tuple(ref[br, rows, :] for ref in results) for br in strided]
        top = functools.reduce(jnp.maximum, [p[2] for p in parts])
        es = [jnp.exp2(p[2] - top) for p in parts]
        num = functools.reduce(jnp.add, [p[0] * e for p, e in zip(parts, es)])
        den = functools.reduce(jnp.add, [p[1] * e for p, e in zip(parts, es)])
        o_ref[rows, :] = (num / den).astype(BF16)

    order = sorted(range(nbr), key=lambda br: B_BRANCHES[br][1] == 1)
    strided = [br for br in order if B_BRANCHES[br][1] != 1]
    for br in order:
        window, d = B_BRANCHES[br]
        assert window // d == BLOCK
        if d == 1:
            kprev, vprev = knp_ref[...], vnp_ref[...]
            for j in range(B_TILE // BLOCK):
                rows = slice(j * BLOCK, (j + 1) * BLOCK)
                kcur, vcur = kn_ref[rows, :], vn_ref[rows, :]
                merge(rows, run(qn_ref[rows, :], kprev, kcur, vprev, vcur,
                                bias_ref[no_prev if j == 0 else 0]))
                kprev, vprev = kcur, vcur
        elif d == B_SPLIT:
            for c in range(B_SPLIT):
                kprev, vprev = k4p_ref[c, sub - BLOCK:sub, :], v4p_ref[c, sub - BLOCK:sub, :]
                for j in range(sub // BLOCK):
                    rows = slice(j * BLOCK, (j + 1) * BLOCK)
                    kcur, vcur = k4_ref[c, rows, :], v4_ref[c, rows, :]
                    res = run(q4_ref[c, rows, :], kprev, kcur, vprev, vcur,
                              bias_ref[no_prev if j == 0 else 0])
                    for ref, val in zip(results, res):
                        ref[br, pl.ds(j * BLOCK * d + c, BLOCK, stride=d), :] = val
                    kprev, vprev = kcur, vcur
        else:
            inner = d // B_SPLIT
            assert d % B_SPLIT == 0 and sub == BLOCK * inner
            for c in range(B_SPLIT):
                for ri in range(inner):
                    rows = pl.ds(ri, BLOCK, stride=inner)
                    res = run(q4_ref[c, rows, :], k4p_ref[c, rows, :], k4_ref[c, rows, :],
                              v4p_ref[c, rows, :], v4_ref[c, rows, :], bias_ref[no_prev])
                    for a, val in enumerate(res):
                        cls_ref[a, c, rows, :] = val
            for a, ref in enumerate(results):
                for c in range(B_SPLIT):
                    ref[br, pl.ds(c, sub, stride=B_SPLIT), :] = cls_ref[a, c]


def _attn_b(qn, kn, vn, q4, k4, v4, *, batch, seq):
    tiles = seq // B_TILE
    sub = B_TILE // B_SPLIT
    tile = lambda b, i: b * tiles + i
    nat = pl.BlockSpec((None, B_TILE, LANES), lambda b, i, s: (s, tile(b, i), 0))
    halo = pl.BlockSpec((None, BLOCK, LANES),
                        lambda b, i, s: (s, jnp.maximum(tile(b, i) * (B_TILE // BLOCK) - 1, 0), 0))
    cls = pl.BlockSpec((None, None, B_SPLIT, sub, LANES), lambda b, i, s: (s, tile(b, i), 0, 0, 0))
    cls_prev = pl.BlockSpec((None, None, B_SPLIT, sub, LANES),
                            lambda b, i, s: (s, jnp.maximum(tile(b, i) - 1, 0), 0, 0, 0))
    nbr = len(B_BRANCHES)
    return pl.pallas_call(
        _attn_b_kernel,
        grid=(batch, tiles, B_SLABS),
        in_specs=[nat, nat, halo, nat, halo, cls, cls, cls_prev, cls, cls_prev,
                  _resident((2, 2 * BLOCK, 2 * BLOCK), lambda b, i, s: (0, 0, 0))],
        out_specs=nat,
        out_shape=jax.ShapeDtypeStruct((B_SLABS, batch * seq, LANES), BF16),
        scratch_shapes=[pltpu.VMEM((nbr, B_TILE, LANES), F32)] * 3
                       + [pltpu.VMEM((3, B_SPLIT, sub, LANES), F32)],
        compiler_params=_params(("parallel", "parallel", "parallel")),
        name="attn_b",
    )(qn, kn, kn, vn, vn, q4, k4, k4, v4, v4, _band_bias(BLOCK))


def _a_head_order():
    group = A_Q_HEADS // A_KV_HEADS
    return [kv * group + m for m in range(group) for kv in range(A_KV_HEADS)]


def _a_column_perm():
    cols = []
    for h in _a_head_order():
        cols.extend(range(h * HEAD_DIM, (h + 1) * HEAD_DIM))
    return jnp.array(cols, dtype=jnp.int32)


def _block_diag(w):
    depth, blocks, n, _ = w.shape
    eye = jnp.eye(blocks, dtype=w.dtype)
    return jnp.einsum('lhij,hg->lhigj', w, eye).reshape(depth, blocks * n, blocks * n)


def kernel(x, positions, norm_ffn1, ffn1_gate, ffn1_up, ffn1_down, norm_mix, w_in, attn_sinks,
           conv_w, conv_b, rg_w_r, rg_b_r, rg_w_i, rg_b_i, rg_lambda, w_out,
           norm_ffn2, ffn2_gate, ffn2_up, ffn2_down, norm_final):
    batch, seq, _ = x.shape
    depth = w_in.shape[0]
    n = batch * seq
    assert seq % B_TILE == 0
    tm = 512

    perm = _a_column_perm()
    a_end = A_WIDTH + 2 * A_KV_WIDTH
    w_in_b = jnp.concatenate([w_in[:, :, :A_WIDTH][:, :, perm] * (SCALE * LOG2E),
                              w_in[:, :, A_WIDTH:a_end],
                              w_in[:, :, a_end:a_end + B_WIDTH] * (SCALE * LOG2E),
                              w_in[:, :, a_end + B_WIDTH:]], axis=2).astype(BF16)
    w_out_b = jnp.concatenate([w_out[:, :A_WIDTH][:, perm], w_out[:, A_WIDTH:]], axis=1).astype(BF16)
    sinks = jnp.repeat(attn_sinks[:, jnp.array(_a_head_order())], HEAD_DIM, axis=1)
    sinks = sinks.reshape(depth, A_SLABS, LANES) * LOG2E
    w_gates = jnp.concatenate([_block_diag(rg_w_r), _block_diag(rg_w_i)], axis=2).astype(BF16)
    b_gates = jnp.concatenate([rg_b_r.reshape(depth, 1, C_WIDTH), rg_b_i.reshape(depth, 1, C_WIDTH)], axis=2)
    ffn_w = [(g.astype(BF16), u.astype(BF16), d.astype(BF16))
             for g, u, d in ((ffn1_gate, ffn1_up, ffn1_down), (ffn2_gate, ffn2_up, ffn2_down))]
    vec = lambda v: v.reshape(depth, 1, -1)
    gf = norm_final.reshape(1, D_MODEL)

    cos, sin, keep = _rope_tables(positions)
    h = x.reshape(n, D_MODEL)
    for l in range(depth):
        h = _ffn(h, vec(norm_ffn1), *ffn_w[0], gf, l, final=False, tm=tm)
        qa, ka, va, qb, kb, vb, q4, k4, v4, oc = _inproj(
            h, vec(norm_mix), w_in_b, cos, sin, keep, conv_w, vec(conv_b), w_gates, b_gates,
            vec(rg_lambda), l, tm=tm, seq=seq)
        oa = _attn_a(qa, ka, va, sinks[l], batch=batch, seq=seq, tq=tm)
        ob = _attn_b(qb, kb, vb, q4, k4, v4, batch=batch, seq=seq)
        h = _ffn(h, vec(norm_ffn2), *ffn_w[1], gf, l, final=(l == depth - 1), tm=tm,
                 mix=(oa, ob, oc, w_out_b))
    return h.reshape(batch, seq, D_MODEL)
```

```python
import functools

import jax
import jax.numpy as jnp
from jax import lax
from jax.experimental import pallas as pl
from jax.experimental.pallas import tpu as pltpu

D_MODEL = 1024
HEAD_DIM = 64
A_Q_HEADS = 4
A_KV_HEADS = 2
A_WINDOW = 128
B_HEADS = 6
B_BRANCHES = ((128, 1), (512, 4), (2048, 16))
C_WIDTH = 384
C_BLOCKS = 6
C_CONV = 4
C_EXP = 8.0
D_FF = 2816
BLOCK = 128
ROPE_THETA = 10000.0
EPS = 1e-6
SCALE = HEAD_DIM ** -0.5
LOG2E = 1.4426950408889634

A_WIDTH = A_Q_HEADS * HEAD_DIM
A_KV_WIDTH = A_KV_HEADS * HEAD_DIM
B_WIDTH = B_HEADS * HEAD_DIM
MIX_WIDTH = A_WIDTH + B_WIDTH + C_WIDTH
IN_COLS = A_WIDTH + 2 * A_KV_WIDTH + 3 * B_WIDTH + 2 * C_WIDTH

LANES = 128
SUBLANES = 8
FF_CHUNK = 256
N_FF_CHUNKS = D_FF // FF_CHUNK
B_SLABS = B_WIDTH // LANES
A_SLABS = A_WIDTH // LANES
MAX_DIL = max(d for _, d in B_BRANCHES)
B_TILE = BLOCK * MAX_DIL
B_SPLIT = 4
VMEM_LIMIT = 56 * 1024 * 1024

F32 = jnp.float32
BF16 = jnp.bfloat16


def _params(sem):
    return pltpu.CompilerParams(dimension_semantics=sem, vmem_limit_bytes=VMEM_LIMIT)


def _resident(shape, index_map):
    return pl.BlockSpec(shape, index_map, pipeline_mode=pl.Buffered(1))


def _rms(x, g):
    return x * lax.rsqrt(jnp.mean(x * x, axis=-1, keepdims=True) + EPS) * g


def _tables_kernel(pos_ref, inv_ref, cos_ref, sin_ref, keep_ref):
    inv = inv_ref[...]
    reps = LANES // HEAD_DIM
    for r in range(pos_ref.shape[0]):
        p = pos_ref[r:r + 1, :].astype(F32)
        ang_t = inv * p
        cos_t, sin_t = jnp.cos(ang_t), jnp.sin(ang_t)
        rows = pl.ds(r * LANES, LANES)
        cos_ref[rows, :] = jnp.concatenate([cos_t, cos_t] * reps, axis=0).T
        sin_ref[rows, :] = jnp.concatenate([-sin_t, sin_t] * reps, axis=0).T
        keep_ref[rows, :] = jnp.where(jnp.broadcast_to(p, (LANES, LANES)) == 0.0, 0.0, 1.0).T


def _rope_tables(positions):
    n = positions.size
    rows = SUBLANES
    pos2 = positions.reshape(n // LANES, LANES)
    half = HEAD_DIM // 2
    inv = 1.0 / (ROPE_THETA ** (jnp.arange(0, HEAD_DIM, 2, dtype=F32) / HEAD_DIM))
    tab = jax.ShapeDtypeStruct((n, LANES), F32)
    out_spec = pl.BlockSpec((rows * LANES, LANES), lambda i: (i, 0))
    return pl.pallas_call(
        _tables_kernel,
        grid=(n // (rows * LANES),),
        in_specs=[pl.BlockSpec((rows, LANES), lambda i: (i, 0)),
                  pl.BlockSpec((half, 1), lambda i: (0, 0))],
        out_specs=[out_spec, out_spec, out_spec],
        out_shape=[tab, tab, tab],
        compiler_params=_params(("arbitrary",)),
        name="rope_tables",
    )(pos2, inv.reshape(half, 1))


def _ffn_kernel(*refs, mixed, final):
    if mixed:
        x_in_ref, a_ref, b_ref, c_ref, wo_ref, *refs, x_ref = refs
        mix = jnp.concatenate([a_ref[...]] + [b_ref[s] for s in range(B_SLABS)] + [c_ref[...]], axis=1)
        x_ref[...] = x_in_ref[...] + jnp.dot(mix, wo_ref[...], preferred_element_type=F32)
    else:
        x_ref, *refs = refs
    g_ref, wg_ref, wu_ref, wd_ref, gf_ref, o_ref, xn_ref, acc_ref = refs
    xn_ref[...] = _rms(x_ref[...], g_ref[...]).astype(BF16)
    for c in range(N_FF_CHUNKS):
        xn = xn_ref[...]
        cols = slice(c * FF_CHUNK, (c + 1) * FF_CHUNK)
        g = jnp.dot(xn, wg_ref[:, cols], preferred_element_type=F32)
        u = jnp.dot(xn, wu_ref[:, cols], preferred_element_type=F32)
        h = (g * (1.0 / (1.0 + jnp.exp(-g))) * u).astype(BF16)
        d = jnp.dot(h, wd_ref[cols, :], preferred_element_type=F32)
        if c == 0:
            acc_ref[...] = d
        elif c < N_FF_CHUNKS - 1:
            acc_ref[...] += d
        else:
            y = x_ref[...] + 0.5 * (acc_ref[...] + d)
            if final:
                y = _rms(y, gf_ref[...])
            o_ref[...] = y


def _ffn(x, g, wg, wu, wd, gf, layer, *, final, tm, mix=None):
    n = x.shape[0]
    row = lambda width: pl.BlockSpec((tm, width), lambda i: (i, 0))
    per_layer = lambda rows, cols: _resident((None, rows, cols), lambda i: (layer, 0, 0))
    in_specs = [row(D_MODEL)]
    args = [x]
    scratch = [pltpu.VMEM((tm, D_MODEL), BF16), pltpu.VMEM((tm, D_MODEL), F32)]
    if mix is not None:
        in_specs += [row(A_WIDTH), pl.BlockSpec((B_SLABS, tm, LANES), lambda i: (0, i, 0)),
                     row(C_WIDTH), per_layer(MIX_WIDTH, D_MODEL)]
        args += list(mix)
        scratch.append(pltpu.VMEM((tm, D_MODEL), F32))
    in_specs += [per_layer(1, D_MODEL), per_layer(D_MODEL, D_FF), per_layer(D_MODEL, D_FF),
                 per_layer(D_FF, D_MODEL), _resident((1, D_MODEL), lambda i: (0, 0))]
    args += [g, wg, wu, wd, gf]
    return pl.pallas_call(
        functools.partial(_ffn_kernel, mixed=mix is not None, final=final),
        grid=(n // tm,),
        in_specs=in_specs,
        out_specs=row(D_MODEL),
        out_shape=jax.ShapeDtypeStruct((n, D_MODEL), F32),
        scratch_shapes=scratch,
        compiler_params=_params(("parallel",)),
        name="ffn_mix" if mix is not None else "ffn",
    )(*args)


def _rope(v, cos, sin_signed, first_half):
    rot = jnp.where(first_half, pltpu.roll(v, LANES - HEAD_DIM // 2, axis=1),
                    pltpu.roll(v, HEAD_DIM // 2, axis=1))
    return v * cos + rot * sin_signed


def _inproj_kernel(x_ref, g_ref, w_ref, cos_ref, sin_ref, keep_ref, cw_ref, cb_ref, wgate_ref,
                   bgate_ref, lam_ref,
                   qa_ref, ka_ref, va_ref, qb_ref, kb_ref, vb_ref, q4_ref, k4_ref, v4_ref, oc_ref,
                   tmp_ref, xbuf_ref, a_ref, b_ref, cin_ref, pc_ref, *, tiles_per_seq):
    step = pl.program_id(0)

    @pl.when(step == 0)
    def _():
        pc_ref[...] = jnp.zeros_like(pc_ref)
        cin_ref[...] = jnp.zeros_like(cin_ref)
        xbuf_ref[...] = jnp.zeros_like(xbuf_ref)

    @pl.when((step + tiles_per_seq - 1) % tiles_per_seq == 0)
    def _():
        cin_ref[...] = jnp.zeros_like(cin_ref)
        xbuf_ref[0:SUBLANES, :] = jnp.zeros((SUBLANES, C_WIDTH), F32)

    xc_lag, gate_lag = pc_ref[:, :C_WIDTH], pc_ref[:, C_WIDTH:]

    xn = _rms(x_ref[...], g_ref[...]).astype(BF16)
    cos = cos_ref[...]
    sin = sin_ref[...]
    first_half = lax.broadcasted_iota(jnp.int32, cos.shape, 1) % HEAD_DIM < HEAD_DIM // 2

    group = 4 * LANES
    groups = [jnp.dot(xn, w_ref[:, c:min(c + group, IN_COLS)], preferred_element_type=F32)
              for c in range(0, IN_COLS, group)]

    def proj(col, width):
        parts = [groups[c // group][:, c % group:c % group + LANES] for c in range(col, col + width, LANES)]
        return parts[0] if len(parts) == 1 else jnp.concatenate(parts, axis=1)

    pc_ref[...] = proj(0, 2 * C_WIDTH)
    col = 2 * C_WIDTH
    for s in range(A_SLABS):
        qa_ref[:, s * LANES:(s + 1) * LANES] = _rope(proj(col, LANES), cos, sin, first_half).astype(BF16)
        col += LANES
    ka_ref[...] = _rope(proj(col, LANES), cos, sin, first_half).astype(BF16)
    col += LANES
    va_ref[...] = proj(col, LANES).astype(BF16)
    col += LANES
    sub = x_ref.shape[0] // B_SPLIT
    for t, (ref, split_ref, rotary) in enumerate(((qb_ref, q4_ref, True), (kb_ref, k4_ref, True),
                                                  (vb_ref, v4_ref, False))):
        for s in range(B_SLABS):
            v = proj(col, LANES)
            if rotary:
                v = _rope(v, cos, sin, first_half)
            ref[s] = v.astype(BF16)
            tmp_ref[t * B_SLABS + s] = v
            for c in range(B_SPLIT):
                split_ref[s, c] = tmp_ref[t * B_SLABS + s, pl.ds(c, sub, stride=B_SPLIT), :]
            col += LANES
    oc_ref[...] = _rglru_tile(xc_lag, gate_lag, keep_ref[...],
                              cw_ref, cb_ref, wgate_ref, bgate_ref, lam_ref,
                              xbuf_ref, a_ref, b_ref, cin_ref).astype(BF16)


def _rglru_tile(xc, gate, keep, cw_ref, cb_ref, wg_ref, bg_ref, lam_ref,
                xbuf_ref, a_ref, b_ref, cin_ref):
    tm = xc.shape[0]

    xbuf_ref[SUBLANES:, :] = xc
    y = cb_ref[...] + cw_ref[0:1, :] * xc
    for j in range(1, C_CONV):
        y = y + cw_ref[j:j + 1, :] * xbuf_ref[pl.ds(SUBLANES - j, tm), :]
    xbuf_ref[0:SUBLANES, :] = xbuf_ref[tm:tm + SUBLANES, :]

    gates = jnp.dot(y.astype(BF16), wg_ref[...], preferred_element_type=F32) + bg_ref[...]
    gates = 1.0 / (1.0 + jnp.exp(-gates))
    r, ig = gates[:, :C_WIDTH], gates[:, C_WIDTH:]
    lam = lam_ref[...]
    softplus = jnp.maximum(-lam, 0.0) + jnp.log(1.0 + jnp.exp(-jnp.abs(lam)))
    a = jnp.exp(-C_EXP * r * softplus)
    keep = jnp.concatenate([keep] * (C_WIDTH // LANES), axis=1) != 0.0
    gap = 1.0 - a * a
    root = jnp.where(gap > 0.0, gap * lax.rsqrt(gap), 0.0)
    mult = jnp.where(keep, root, 1.0)
    a = jnp.where(keep, a, 0.0)
    b = mult * (ig * y)

    in_group = lax.broadcasted_iota(jnp.int32, a.shape, 0) % SUBLANES
    sh = 1
    while sh < SUBLANES:
        ok = in_group >= sh
        b = b + a * jnp.where(ok, pltpu.roll(b, sh, axis=0), 0.0)
        a = a * jnp.where(ok, pltpu.roll(a, sh, axis=0), 1.0)
        sh *= 2

    groups = tm // SUBLANES
    grp = lax.broadcasted_iota(jnp.int32, (groups, LANES), 0)
    last = SUBLANES - 1
    slabs = []
    for s in range(C_WIDTH // LANES):
        lanes = slice(s * LANES, (s + 1) * LANES)
        a_ref[s] = a[:, lanes]
        b_ref[s] = b[:, lanes]
        a_end = a_ref[s, pl.ds(last, groups, stride=SUBLANES), :]
        b_end = b_ref[s, pl.ds(last, groups, stride=SUBLANES), :]
        b_end = b_end + jnp.where(grp == 0, a_end * cin_ref[s, last:SUBLANES, :], 0.0)
        sh = 1
        while sh < groups:
            ok = grp >= sh
            b_end = b_end + a_end * jnp.where(ok, pltpu.roll(b_end, sh, axis=0), 0.0)
            a_end = a_end * jnp.where(ok, pltpu.roll(a_end, sh, axis=0), 1.0)
            sh *= 2
        cin_ref[s, SUBLANES:, :] = b_end
        hs = []
        for g in range(groups):
            rows = slice(g * SUBLANES, (g + 1) * SUBLANES)
            hs.append(b_ref[s, rows, :] + a_ref[s, rows, :] * cin_ref[s, last + g:SUBLANES + g, :])
        slabs.append(jnp.concatenate(hs, axis=0))
        cin_ref[s, last:SUBLANES, :] = b_end[groups - 1:groups, :]
    h = jnp.concatenate(slabs, axis=1)

    gelu = 0.5 * gate * (1.0 + jnp.tanh(0.7978845608028654 * (gate + 0.044715 * gate * gate * gate)))
    return h * gelu


def _inproj(x, g, w, cos, sin, keep, cw, cb, wgate, bgate, lam, layer, *, tm, seq):
    n = x.shape[0]
    tiles = n // tm
    n_slabs = C_WIDTH // LANES
    cur = lambda i: jnp.minimum(i, tiles - 1)
    lag = lambda i: jnp.maximum(i - 1, 0)
    row = lambda width: pl.BlockSpec((tm, width), lambda i: (cur(i), 0))
    lag_row = lambda width: pl.BlockSpec((tm, width), lambda i: (lag(i), 0))
    slab = pl.BlockSpec((B_SLABS, tm, LANES), lambda i: (0, cur(i), 0))
    per_tile = B_TILE // tm
    split = pl.BlockSpec((B_SLABS, None, B_SPLIT, tm // B_SPLIT, LANES),
                         lambda i: (0, cur(i) // per_tile, 0, cur(i) % per_tile, 0))
    sds = jax.ShapeDtypeStruct
    nat = sds((B_SLABS, n, LANES), BF16)
    cls = sds((B_SLABS, n // B_TILE, B_SPLIT, B_TILE // B_SPLIT, LANES), F32)
    per_layer = lambda rows, cols: _resident((None, rows, cols), lambda i: (layer, 0, 0))
    return pl.pallas_call(
        functools.partial(_inproj_kernel, tiles_per_seq=seq // tm),
        grid=(tiles + 1,),
        in_specs=[row(D_MODEL), per_layer(1, D_MODEL), per_layer(D_MODEL, IN_COLS),
                  row(LANES), row(LANES), lag_row(LANES),
                  per_layer(C_CONV, C_WIDTH), per_layer(1, C_WIDTH),
                  per_layer(C_WIDTH, 2 * C_WIDTH), per_layer(1, 2 * C_WIDTH), per_layer(1, C_WIDTH)],
        out_specs=[row(A_WIDTH), row(A_KV_WIDTH), row(A_KV_WIDTH), slab, slab, slab,
                   split, split, split, lag_row(C_WIDTH)],
        out_shape=[sds((n, A_WIDTH), BF16), sds((n, A_KV_WIDTH), BF16), sds((n, A_KV_WIDTH), BF16),
                   nat, nat, nat, cls, cls, cls, sds((n, C_WIDTH), BF16)],
        scratch_shapes=[pltpu.VMEM((3 * B_SLABS, tm, LANES), F32),
                        pltpu.VMEM((tm + SUBLANES, C_WIDTH), F32),
                        pltpu.VMEM((n_slabs, tm, LANES), F32), pltpu.VMEM((n_slabs, tm, LANES), F32),
                        pltpu.VMEM((n_slabs, SUBLANES + tm // SUBLANES, LANES), F32),
                        pltpu.VMEM((tm, 2 * C_WIDTH), F32)],
        compiler_params=_params(("arbitrary",)),
        name="inproj",
    )(x, g, w, cos, sin, keep, cw, cb, wgate, bgate, lam)


def _band_bias(max_dist):
    row = lax.broadcasted_iota(jnp.int32, (2 * BLOCK, 2 * BLOCK), 0) % BLOCK
    col = lax.broadcasted_iota(jnp.int32, (2 * BLOCK, 2 * BLOCK), 1)
    first_col = row + (BLOCK - max_dist)
    out = []
    for lo in (first_col, jnp.maximum(first_col, BLOCK)):
        out.append(jnp.where((col >= lo) & (col <= row + BLOCK), 0.0, -jnp.inf).astype(F32))
    return jnp.stack(out)


def _attn_unit(q, kcat, vcat, bias):
    lane = lax.broadcasted_iota(jnp.int32, (BLOCK, LANES), 1)
    left = lane < HEAD_DIM
    zero = jnp.zeros_like(q)
    q2 = jnp.concatenate([jnp.where(left, q, zero), jnp.where(left, zero, q)], axis=0)
    s = lax.dot_general(q2, kcat, (((1,), (1,)), ((), ())), preferred_element_type=F32) + bias
    m = jnp.max(s, axis=-1, keepdims=True)
    p = jnp.exp2(s - m).astype(BF16)
    v1 = jnp.concatenate([vcat, jnp.ones_like(vcat)], axis=1)
    r = jnp.dot(p, v1, preferred_element_type=F32)
    num = jnp.where(left, r[:BLOCK, :LANES], r[BLOCK:, :LANES])
    den = jnp.where(left, r[:BLOCK, LANES:], r[BLOCK:, LANES:])
    mx = jnp.where(left, jnp.broadcast_to(m[:BLOCK], (BLOCK, LANES)),
                   jnp.broadcast_to(m[BLOCK:], (BLOCK, LANES)))
    return num, den, mx


def _attn_a_kernel(q_ref, kc_ref, kp_ref, vc_ref, vp_ref, sink_ref, bias_ref, o_ref, *, tq):
    no_prev = jnp.where(pl.program_id(1) == 0, 1, 0)
    for j in range(tq // BLOCK):
        rows = slice(j * BLOCK, (j + 1) * BLOCK)
        if j == 0:
            kprev, vprev = kp_ref[...], vp_ref[...]
            bias = bias_ref[no_prev]
        else:
            prev_rows = slice((j - 1) * BLOCK, j * BLOCK)
            kprev, vprev = kc_ref[prev_rows, :], vc_ref[prev_rows, :]
            bias = bias_ref[0]
        kcat = jnp.concatenate([kprev, kc_ref[rows, :]], axis=0)
        vcat = jnp.concatenate([vprev, vc_ref[rows, :]], axis=0)
        for s in range(A_SLABS):
            lanes = slice(s * LANES, (s + 1) * LANES)
            num, den, mx = _attn_unit(q_ref[rows, lanes], kcat, vcat, bias)
            o_ref[rows, lanes] = (num / (den + jnp.exp2(sink_ref[s:s + 1, :] - mx))).astype(BF16)


def _attn_a(qa, ka, va, sinks, *, batch, seq, tq):
    tiles = seq // tq
    ratio = tq // BLOCK
    cur = lambda width: pl.BlockSpec((tq, width), lambda b, i: (b * tiles + i, 0))
    prev = pl.BlockSpec((BLOCK, A_KV_WIDTH),
                        lambda b, i: (jnp.maximum((b * tiles + i) * ratio - 1, 0), 0))
    return pl.pallas_call(
        functools.partial(_attn_a_kernel, tq=tq),
        grid=(batch, tiles),
        in_specs=[cur(A_WIDTH), cur(A_KV_WIDTH), prev, cur(A_KV_WIDTH), prev,
                  _resident((A_SLABS, LANES), lambda b, i: (0, 0)),
                  _resident((2, 2 * BLOCK, 2 * BLOCK), lambda b, i: (0, 0, 0))],
        out_specs=cur(A_WIDTH),
        out_shape=jax.ShapeDtypeStruct((batch * seq, A_WIDTH), BF16),
        compiler_params=_params(("parallel", "parallel")),
        name="attn_a",
    )(qa, ka, ka, va, va, sinks, _band_bias(A_WINDOW - 1))


def _attn_b_kernel(qn_ref, kn_ref, knp_ref, vn_ref, vnp_ref, q4_ref, k4_ref, k4p_ref, v4_ref, v4p_ref,
                   bias_ref, o_ref, num_ref, den_ref, mx_ref, cls_ref):
    no_prev = jnp.where(pl.program_id(1) == 0, 1, 0)
    nbr = len(B_BRANCHES)
    sub = B_TILE // B_SPLIT
    results = (num_ref, den_ref, mx_ref)

    def run(q, kprev, kcur, vprev, vcur, bias):
        return _attn_unit(q.astype(BF16), jnp.concatenate([kprev, kcur], axis=0).astype(BF16),
                          jnp.concatenate([vprev, vcur], axis=0).astype(BF16), bias)

    def merge(rows, first):
        parts = [first] + [tuple(ref[br, rows, :] for ref in results) for br in strided]
        top = functools.reduce(jnp.maximum, [p[2] for p in parts])
        es = [jnp.exp2(p[2] - top) for p in parts]
        num = functools.reduce(jnp.add, [p[0] * e for p, e in zip(parts, es)])
        den = functools.reduce(jnp.add, [p[1] * e for p, e in zip(parts, es)])
        o_ref[rows, :] = (num / den).astype(BF16)

    order = sorted(range(nbr), key=lambda br: B_BRANCHES[br][1] == 1)
    strided = [br for br in order if B_BRANCHES[br][1] != 1]
    for br in order:
        window, d = B_BRANCHES[br]
        assert window // d == BLOCK
        if d == 1:
            kprev, vprev = knp_ref[...], vnp_ref[...]
            for j in range(B_TILE // BLOCK):
                rows = slice(j * BLOCK, (j + 1) * BLOCK)
                kcur, vcur = kn_ref[rows, :], vn_ref[rows, :]
                merge(rows, run(qn_ref[rows, :], kprev, kcur, vprev, vcur,
                                bias_ref[no_prev if j == 0 else 0]))
                kprev, vprev = kcur, vcur
        elif d == B_SPLIT:
            for c in range(B_SPLIT):
                kprev, vprev = k4p_ref[c, sub - BLOCK:sub, :], v4p_ref[c, sub - BLOCK:sub, :]
                for j in range(sub // BLOCK):
                    rows = slice(j * BLOCK, (j + 1) * BLOCK)
                    kcur, vcur = k4_ref[c, rows, :], v4_ref[c, rows, :]
                    res = run(q4_ref[c, rows, :], kprev, kcur, vprev, vcur,
                              bias_ref[no_prev if j == 0 else 0])
                    for ref, val in zip(results, res):
                        ref[br, pl.ds(j * BLOCK * d + c, BLOCK, stride=d), :] = val
                    kprev, vprev = kcur, vcur
        else:
            inner = d // B_SPLIT
            assert d % B_SPLIT == 0 and sub == BLOCK * inner
            for c in range(B_SPLIT):
                for ri in range(inner):
                    rows = pl.ds(ri, BLOCK, stride=inner)
                    res = run(q4_ref[c, rows, :], k4p_ref[c, rows, :], k4_ref[c, rows, :],
                              v4p_ref[c, rows, :], v4_ref[c, rows, :], bias_ref[no_prev])
                    for a, val in enumerate(res):
                        cls_ref[a, c, rows, :] = val
            for a, ref in enumerate(results):
                for c in range(B_SPLIT):
                    ref[br, pl.ds(c, sub, stride=B_SPLIT), :] = cls_ref[a, c]


def _attn_b(qn, kn, vn, q4, k4, v4, *, batch, seq):
    tiles = seq // B_TILE
    sub = B_TILE // B_SPLIT
    tile = lambda b, i: b * tiles + i
    nat = pl.BlockSpec((None, B_TILE, LANES), lambda b, i, s: (s, tile(b, i), 0))
    halo = pl.BlockSpec((None, BLOCK, LANES),
                        lambda b, i, s: (s, jnp.maximum(tile(b, i) * (B_TILE // BLOCK) - 1, 0), 0))
    cls = pl.BlockSpec((None, None, B_SPLIT, sub, LANES), lambda b, i, s: (s, tile(b, i), 0, 0, 0))
    cls_prev = pl.BlockSpec((None, None, B_SPLIT, sub, LANES),
                            lambda b, i, s: (s, jnp.maximum(tile(b, i) - 1, 0), 0, 0, 0))
    nbr = len(B_BRANCHES)
    return pl.pallas_call(
        _attn_b_kernel,
        grid=(batch, tiles, B_SLABS),
        in_specs=[nat, nat, halo, nat, halo, cls, cls, cls_prev, cls, cls_prev,
                  _resident((2, 2 * BLOCK, 2 * BLOCK), lambda b, i, s: (0, 0, 0))],
        out_specs=nat,
        out_shape=jax.ShapeDtypeStruct((B_SLABS, batch * seq, LANES), BF16),
        scratch_shapes=[pltpu.VMEM((nbr, B_TILE, LANES), F32)] * 3
                       + [pltpu.VMEM((3, B_SPLIT, sub, LANES), F32)],
        compiler_params=_params(("parallel", "parallel", "parallel")),
        name="attn_b",
    )(qn, kn, kn, vn, vn, q4, k4, k4, v4, v4, _band_bias(BLOCK))


def _a_head_order():
    group = A_Q_HEADS // A_KV_HEADS
    return [kv * group + m for m in range(group) for kv in range(A_KV_HEADS)]


def _a_column_perm():
    cols = []
    for h in _a_head_order():
        cols.extend(range(h * HEAD_DIM, (h + 1) * HEAD_DIM))
    return jnp.array(cols, dtype=jnp.int32)


def _block_diag(w):
    depth, blocks, n, _ = w.shape
    eye = jnp.eye(blocks, dtype=w.dtype)
    return jnp.einsum('lhij,hg->lhigj', w, eye).reshape(depth, blocks * n, blocks * n)


def kernel(x, positions, norm_ffn1, ffn1_gate, ffn1_up, ffn1_down, norm_mix, w_in, attn_sinks,
           conv_w, conv_b, rg_w_r, rg_b_r, rg_w_i, rg_b_i, rg_lambda, w_out,
           norm_ffn2, ffn2_gate, ffn2_up, ffn2_down, norm_final):
    batch, seq, _ = x.shape
    depth = w_in.shape[0]
    n = batch * seq
    assert seq % B_TILE == 0
    tm = 512

    perm = _a_column_perm()
    a_end = A_WIDTH + 2 * A_KV_WIDTH
    b_end = a_end + 3 * B_WIDTH
    w_in_b = jnp.concatenate([w_in[:, :, b_end:],
                              w_in[:, :, :A_WIDTH][:, :, perm] * (SCALE * LOG2E),
                              w_in[:, :, A_WIDTH:a_end],
                              w_in[:, :, a_end:a_end + B_WIDTH] * (SCALE * LOG2E),
                              w_in[:, :, a_end + B_WIDTH:b_end]], axis=2).astype(BF16)
    w_out_b = jnp.concatenate([w_out[:, :A_WIDTH][:, perm], w_out[:, A_WIDTH:]], axis=1).astype(BF16)
    sinks = jnp.repeat(attn_sinks[:, jnp.array(_a_head_order())], HEAD_DIM, axis=1)
    sinks = sinks.reshape(depth, A_SLABS, LANES) * LOG2E
    w_gates = jnp.concatenate([_block_diag(rg_w_r), _block_diag(rg_w_i)], axis=2).astype(BF16)
    b_gates = jnp.concatenate([rg_b_r.reshape(depth, 1, C_WIDTH), rg_b_i.reshape(depth, 1, C_WIDTH)], axis=2)
    ffn_w = [(g.astype(BF16), u.astype(BF16), d.astype(BF16))
             for g, u, d in ((ffn1_gate, ffn1_up, ffn1_down), (ffn2_gate, ffn2_up, ffn2_down))]
    vec = lambda v: v.reshape(depth, 1, -1)
    gf = norm_final.reshape(1, D_MODEL)

    cos, sin, keep = _rope_tables(positions)
    h = x.reshape(n, D_MODEL)
    for l in range(depth):
        h = _ffn(h, vec(norm_ffn1), *ffn_w[0], gf, l, final=False, tm=tm)
        qa, ka, va, qb, kb, vb, q4, k4, v4, oc = _inproj(
            h, vec(norm_mix), w_in_b, cos, sin, keep, conv_w, vec(conv_b), w_gates, b_gates,
            vec(rg_lambda), l, tm=tm, seq=seq)
        oa = _attn_a(qa, ka, va, sinks[l], batch=batch, seq=seq, tq=tm)
        ob = _attn_b(qb, kb, vb, q4, k4, v4, batch=batch, seq=seq)
        h = _ffn(h, vec(norm_ffn2), *ffn_w[1], gf, l, final=(l == depth - 1), tm=tm,
                 mix=(oa, ob, oc, w_out_b))
    return h.reshape(batch, seq, D_MODEL)
```

```python
import functools

import jax
import jax.numpy as jnp
from jax import lax
from jax.experimental import pallas as pl
from jax.experimental.pallas import tpu as pltpu

D_MODEL = 1024
HEAD_DIM = 64
A_Q_HEADS = 4
A_KV_HEADS = 2
A_WINDOW = 128
B_HEADS = 6
B_BRANCHES = ((128, 1), (512, 4), (2048, 16))
C_WIDTH = 384
C_BLOCKS = 6
C_CONV = 4
C_EXP = 8.0
D_FF = 2816
BLOCK = 128
ROPE_THETA = 10000.0
EPS = 1e-6
SCALE = HEAD_DIM ** -0.5
LOG2E = 1.4426950408889634

A_WIDTH = A_Q_HEADS * HEAD_DIM
A_KV_WIDTH = A_KV_HEADS * HEAD_DIM
B_WIDTH = B_HEADS * HEAD_DIM
MIX_WIDTH = A_WIDTH + B_WIDTH + C_WIDTH
IN_COLS = A_WIDTH + 2 * A_KV_WIDTH + 3 * B_WIDTH + 2 * C_WIDTH

LANES = 128
SUBLANES = 8
FF_CHUNK = 256
N_FF_CHUNKS = D_FF // FF_CHUNK
B_SLABS = B_WIDTH // LANES
A_SLABS = A_WIDTH // LANES
MAX_DIL = max(d for _, d in B_BRANCHES)
B_TILE = BLOCK * MAX_DIL
B_SPLIT = 4
VMEM_LIMIT = 56 * 1024 * 1024

F32 = jnp.float32
BF16 = jnp.bfloat16


def _params(sem):
    return pltpu.CompilerParams(dimension_semantics=sem, vmem_limit_bytes=VMEM_LIMIT)


def _resident(shape, index_map):
    return pl.BlockSpec(shape, index_map, pipeline_mode=pl.Buffered(1))


def _rms(x, g):
    return x * lax.rsqrt(jnp.mean(x * x, axis=-1, keepdims=True) + EPS) * g


def _tables_kernel(pos_ref, inv_ref, cos_ref, sin_ref, keep_ref):
    inv = inv_ref[...]
    reps = LANES // HEAD_DIM
    for r in range(pos_ref.shape[0]):
        p = pos_ref[r:r + 1, :].astype(F32)
        ang_t = inv * p
        cos_t, sin_t = jnp.cos(ang_t), jnp.sin(ang_t)
        rows = pl.ds(r * LANES, LANES)
        cos_ref[rows, :] = jnp.concatenate([cos_t, cos_t] * reps, axis=0).T
        sin_ref[rows, :] = jnp.concatenate([-sin_t, sin_t] * reps, axis=0).T
        keep_ref[rows, :] = jnp.where(jnp.broadcast_to(p, (LANES, LANES)) == 0.0, 0.0, 1.0).T


def _rope_tables(positions):
    n = positions.size
    rows = SUBLANES
    pos2 = positions.reshape(n // LANES, LANES)
    half = HEAD_DIM // 2
    inv = 1.0 / (ROPE_THETA ** (jnp.arange(0, HEAD_DIM, 2, dtype=F32) / HEAD_DIM))
    tab = jax.ShapeDtypeStruct((n, LANES), F32)
    out_spec = pl.BlockSpec((rows * LANES, LANES), lambda i: (i, 0))
    return pl.pallas_call(
        _tables_kernel,
        grid=(n // (rows * LANES),),
        in_specs=[pl.BlockSpec((rows, LANES), lambda i: (i, 0)),
                  pl.BlockSpec((half, 1), lambda i: (0, 0))],
        out_specs=[out_spec, out_spec, out_spec],
        out_shape=[tab, tab, tab],
        compiler_params=_params(("arbitrary",)),
        name="rope_tables",
    )(pos2, inv.reshape(half, 1))


def _ffn_kernel(*refs, mixed, final):
    if mixed:
        x_in_ref, a_ref, b_ref, c_ref, wo_ref, *refs, x_ref = refs
        mix = jnp.concatenate([a_ref[...]] + [b_ref[s] for s in range(B_SLABS)] + [c_ref[...]], axis=1)
        x_ref[...] = x_in_ref[...] + jnp.dot(mix, wo_ref[...], preferred_element_type=F32)
    else:
        x_ref, *refs = refs
    g_ref, wg_ref, wu_ref, wd_ref, gf_ref, o_ref, xn_ref, acc_ref = refs
    xn_ref[...] = _rms(x_ref[...], g_ref[...]).astype(BF16)
    for c in range(N_FF_CHUNKS):
        xn = xn_ref[...]
        cols = slice(c * FF_CHUNK, (c + 1) * FF_CHUNK)
        g = jnp.dot(xn, wg_ref[:, cols], preferred_element_type=F32)
        u = jnp.dot(xn, wu_ref[:, cols], preferred_element_type=F32)
        h = (g * (1.0 / (1.0 + jnp.exp(-g))) * u).astype(BF16)
        d = jnp.dot(h, wd_ref[cols, :], preferred_element_type=F32)
        if c == 0:
            acc_ref[...] = d
        elif c < N_FF_CHUNKS - 1:
            acc_ref[...] += d
        else:
            y = x_ref[...] + 0.5 * (acc_ref[...] + d)
            if final:
                y = _rms(y, gf_ref[...])
            o_ref[...] = y


def _ffn(x, g, wg, wu, wd, gf, layer, *, final, tm, mix=None):
    n = x.shape[0]
    row = lambda width: pl.BlockSpec((tm, width), lambda i: (i, 0))
    per_layer = lambda rows, cols: _resident((None, rows, cols), lambda i: (layer, 0, 0))
    in_specs = [row(D_MODEL)]
    args = [x]
    scratch = [pltpu.VMEM((tm, D_MODEL), BF16), pltpu.VMEM((tm, D_MODEL), F32)]
    if mix is not None:
        in_specs += [row(A_WIDTH), pl.BlockSpec((B_SLABS, tm, LANES), lambda i: (0, i, 0)),
                     row(C_WIDTH), per_layer(MIX_WIDTH, D_MODEL)]
        args += list(mix)
        scratch.append(pltpu.VMEM((tm, D_MODEL), F32))
    in_specs += [per_layer(1, D_MODEL), per_layer(D_MODEL, D_FF), per_layer(D_MODEL, D_FF),
                 per_layer(D_FF, D_MODEL), _resident((1, D_MODEL), lambda i: (0, 0))]
    args += [g, wg, wu, wd, gf]
    return pl.pallas_call(
        functools.partial(_ffn_kernel, mixed=mix is not None, final=final),
        grid=(n // tm,),
        in_specs=in_specs,
        out_specs=row(D_MODEL),
        out_shape=jax.ShapeDtypeStruct((n, D_MODEL), F32),
        scratch_shapes=scratch,
        compiler_params=_params(("parallel",)),
        name="ffn_mix" if mix is not None else "ffn",
    )(*args)


def _rope(v, cos, sin_signed, first_half):
    rot = jnp.where(first_half, pltpu.roll(v, LANES - HEAD_DIM // 2, axis=1),
                    pltpu.roll(v, HEAD_DIM // 2, axis=1))
    return v * cos + rot * sin_signed


def _inproj_kernel(x_ref, g_ref, w_ref, cos_ref, sin_ref, keep_ref, cw_ref, cb_ref, wgate_ref,
                   bgate_ref, lam_ref,
                   qa_ref, ka_ref, va_ref, qb_ref, kb_ref, vb_ref, q4_ref, k4_ref, v4_ref, oc_ref,
                   tmp_ref, xbuf_ref, a_ref, b_ref, cin_ref, pc_ref, *, tiles_per_seq):
    step = pl.program_id(0)

    @pl.when(step == 0)
    def _():
        pc_ref[...] = jnp.zeros_like(pc_ref)
        cin_ref[...] = jnp.zeros_like(cin_ref)
        xbuf_ref[...] = jnp.zeros_like(xbuf_ref)

    @pl.when((step + tiles_per_seq - 1) % tiles_per_seq == 0)
    def _():
        cin_ref[...] = jnp.zeros_like(cin_ref)
        xbuf_ref[0:SUBLANES, :] = jnp.zeros((SUBLANES, C_WIDTH), F32)

    xc_lag, gate_lag = pc_ref[:, :C_WIDTH], pc_ref[:, C_WIDTH:]

    xn = _rms(x_ref[...], g_ref[...]).astype(BF16)
    cos = cos_ref[...]
    sin = sin_ref[...]
    first_half = lax.broadcasted_iota(jnp.int32, cos.shape, 1) % HEAD_DIM < HEAD_DIM // 2

    group = 4 * LANES
    groups = [jnp.dot(xn, w_ref[:, c:min(c + group, IN_COLS)], preferred_element_type=F32)
              for c in range(0, IN_COLS, group)]

    def proj(col, width):
        parts = [groups[c // group][:, c % group:c % group + LANES] for c in range(col, col + width, LANES)]
        return parts[0] if len(parts) == 1 else jnp.concatenate(parts, axis=1)

    pc_ref[...] = proj(0, 2 * C_WIDTH)
    col = 2 * C_WIDTH
    for s in range(A_SLABS):
        qa_ref[:, s * LANES:(s + 1) * LANES] = _rope(proj(col, LANES), cos, sin, first_half).astype(BF16)
        col += LANES
    ka_ref[...] = _rope(proj(col, LANES), cos, sin, first_half).astype(BF16)
    col += LANES
    va_ref[...] = proj(col, LANES).astype(BF16)
    col += LANES
    sub = x_ref.shape[0] // B_SPLIT
    for t, (ref, split_ref, rotary) in enumerate(((qb_ref, q4_ref, True), (kb_ref, k4_ref, True),
                                                  (vb_ref, v4_ref, False))):
        for s in range(B_SLABS):
            v = proj(col, LANES)
            if rotary:
                v = _rope(v, cos, sin, first_half)
            ref[s] = v.astype(BF16)
            tmp_ref[t * B_SLABS + s] = v
            for c in range(B_SPLIT):
                split_ref[s, c] = tmp_ref[t * B_SLABS + s, pl.ds(c, sub, stride=B_SPLIT), :]
            col += LANES
    oc_ref[...] = _rglru_tile(xc_lag, gate_lag, keep_ref[...],
                              cw_ref, cb_ref, wgate_ref, bgate_ref, lam_ref,
                              xbuf_ref, a_ref, b_ref, cin_ref).astype(BF16)


def _rglru_tile(xc, gate, keep, cw_ref, cb_ref, wg_ref, bg_ref, lam_ref,
                xbuf_ref, a_ref, b_ref, cin_ref):
    tm = xc.shape[0]

    xbuf_ref[SUBLANES:, :] = xc
    y = cb_ref[...] + cw_ref[0:1, :] * xc
    for j in range(1, C_CONV):
        y = y + cw_ref[j:j + 1, :] * xbuf_ref[pl.ds(SUBLANES - j, tm), :]
    xbuf_ref[0:SUBLANES, :] = xbuf_ref[tm:tm + SUBLANES, :]

    gates = jnp.dot(y.astype(BF16), wg_ref[...], preferred_element_type=F32) + bg_ref[...]
    gates = 1.0 / (1.0 + jnp.exp(-gates))
    r, ig = gates[:, :C_WIDTH], gates[:, C_WIDTH:]
    lam = lam_ref[...]
    softplus = jnp.maximum(-lam, 0.0) + jnp.log(1.0 + jnp.exp(-jnp.abs(lam)))
    a = jnp.exp(-C_EXP * r * softplus)
    keep = jnp.concatenate([keep] * (C_WIDTH // LANES), axis=1) != 0.0
    gap = 1.0 - a * a
    root = jnp.where(gap > 0.0, gap * lax.rsqrt(gap), 0.0)
    mult = jnp.where(keep, root, 1.0)
    a = jnp.where(keep, a, 0.0)
    b = mult * (ig * y)

    in_group = lax.broadcasted_iota(jnp.int32, a.shape, 0) % SUBLANES
    sh = 1
    while sh < SUBLANES:
        ok = in_group >= sh
        b = b + a * jnp.where(ok, pltpu.roll(b, sh, axis=0), 0.0)
        a = a * jnp.where(ok, pltpu.roll(a, sh, axis=0), 1.0)
        sh *= 2

    groups = tm // SUBLANES
    grp = lax.broadcasted_iota(jnp.int32, (groups, LANES), 0)
    last = SUBLANES - 1
    slabs = []
    for s in range(C_WIDTH // LANES):
        lanes = slice(s * LANES, (s + 1) * LANES)
        a_ref[s] = a[:, lanes]
        b_ref[s] = b[:, lanes]
        a_end = a_ref[s, pl.ds(last, groups, stride=SUBLANES), :]
        b_end = b_ref[s, pl.ds(last, groups, stride=SUBLANES), :]
        b_end = b_end + jnp.where(grp == 0, a_end * cin_ref[s, last:SUBLANES, :], 0.0)
        sh = 1
        while sh < groups:
            ok = grp >= sh
            b_end = b_end + a_end * jnp.where(ok, pltpu.roll(b_end, sh, axis=0), 0.0)
            a_end = a_end * jnp.where(ok, pltpu.roll(a_end, sh, axis=0), 1.0)
            sh *= 2
        cin_ref[s, SUBLANES:, :] = b_end
        hs = []
        for g in range(groups):
            rows = slice(g * SUBLANES, (g + 1) * SUBLANES)
            hs.append(b_ref[s, rows, :] + a_ref[s, rows, :] * cin_ref[s, last + g:SUBLANES + g, :])
        slabs.append(jnp.concatenate(hs, axis=0))
        cin_ref[s, last:SUBLANES, :] = b_end[groups - 1:groups, :]
    h = jnp.concatenate(slabs, axis=1)

    gelu = 0.5 * gate * (1.0 + jnp.tanh(0.7978845608028654 * (gate + 0.044715 * gate * gate * gate)))
    return h * gelu


def _inproj(x, g, w, cos, sin, keep, cw, cb, wgate, bgate, lam, layer, *, tm, seq):
    n = x.shape[0]
    tiles = n // tm
    n_slabs = C_WIDTH // LANES
    cur = lambda i: jnp.minimum(i, tiles - 1)
    lag = lambda i: jnp.maximum(i - 1, 0)
    row = lambda width: pl.BlockSpec((tm, width), lambda i: (cur(i), 0))
    lag_row = lambda width: pl.BlockSpec((tm, width), lambda i: (lag(i), 0))
    slab = pl.BlockSpec((B_SLABS, tm, LANES), lambda i: (0, cur(i), 0))
    per_tile = B_TILE // tm
    split = pl.BlockSpec((B_SLABS, None, B_SPLIT, tm // B_SPLIT, LANES),
                         lambda i: (0, cur(i) // per_tile, 0, cur(i) % per_tile, 0))
    sds = jax.ShapeDtypeStruct
    nat = sds((B_SLABS, n, LANES), BF16)
    cls = sds((B_SLABS, n // B_TILE, B_SPLIT, B_TILE // B_SPLIT, LANES), F32)
    per_layer = lambda rows, cols: _resident((None, rows, cols), lambda i: (layer, 0, 0))
    return pl.pallas_call(
        functools.partial(_inproj_kernel, tiles_per_seq=seq // tm),
        grid=(tiles + 1,),
        in_specs=[row(D_MODEL), per_layer(1, D_MODEL), per_layer(D_MODEL, IN_COLS),
                  row(LANES), row(LANES), lag_row(LANES),
                  per_layer(C_CONV, C_WIDTH), per_layer(1, C_WIDTH),
                  per_layer(C_WIDTH, 2 * C_WIDTH), per_layer(1, 2 * C_WIDTH), per_layer(1, C_WIDTH)],
        out_specs=[row(A_WIDTH), row(A_KV_WIDTH), row(A_KV_WIDTH), slab, slab, slab,
                   split, split, split, lag_row(C_WIDTH)],
        out_shape=[sds((n, A_WIDTH), BF16), sds((n, A_KV_WIDTH), BF16), sds((n, A_KV_WIDTH), BF16),
                   nat, nat, nat, cls, cls, cls, sds((n, C_WIDTH), BF16)],
        scratch_shapes=[pltpu.VMEM((3 * B_SLABS, tm, LANES), F32),
                        pltpu.VMEM((tm + SUBLANES, C_WIDTH), F32),
                        pltpu.VMEM((n_slabs, tm, LANES), F32), pltpu.VMEM((n_slabs, tm, LANES), F32),
                        pltpu.VMEM((n_slabs, SUBLANES + tm // SUBLANES, LANES), F32),
                        pltpu.VMEM((tm, 2 * C_WIDTH), F32)],
        compiler_params=_params(("arbitrary",)),
        name="inproj",
    )(x, g, w, cos, sin, keep, cw, cb, wgate, bgate, lam)


def _band_bias(max_dist):
    row = lax.broadcasted_iota(jnp.int32, (2 * BLOCK, 2 * BLOCK), 0) % BLOCK
    col = lax.broadcasted_iota(jnp.int32, (2 * BLOCK, 2 * BLOCK), 1)
    first_col = row + (BLOCK - max_dist)
    out = []
    for lo in (first_col, jnp.maximum(first_col, BLOCK)):
        out.append(jnp.where((col >= lo) & (col <= row + BLOCK), 0.0, -jnp.inf).astype(F32))
    return jnp.stack(out)


def _attn_unit(q, kcat, vcat, bias):
    lane = lax.broadcasted_iota(jnp.int32, (BLOCK, LANES), 1)
    left = lane < HEAD_DIM
    zero = jnp.zeros_like(q)
    q2 = jnp.concatenate([jnp.where(left, q, zero), jnp.where(left, zero, q)], axis=0)
    s = lax.dot_general(q2, kcat, (((1,), (1,)), ((), ())), preferred_element_type=F32) + bias
    m = jnp.max(s, axis=-1, keepdims=True)
    p = jnp.exp2(s - m).astype(BF16)
    v1 = jnp.concatenate([vcat, jnp.ones_like(vcat)], axis=1)
    r = jnp.dot(p, v1, preferred_element_type=F32)
    num = jnp.where(left, r[:BLOCK, :LANES], r[BLOCK:, :LANES])
    den = jnp.where(left, r[:BLOCK, LANES:], r[BLOCK:, LANES:])
    mx = jnp.where(left, jnp.broadcast_to(m[:BLOCK], (BLOCK, LANES)),
                   jnp.broadcast_to(m[BLOCK:], (BLOCK, LANES)))
    return num, den, mx


def _attn_a_kernel(q_ref, kc_ref, kp_ref, vc_ref, vp_ref, sink_ref, bias_ref, o_ref, *, tq):
    no_prev = jnp.where(pl.program_id(1) == 0, 1, 0)
    for j in range(tq // BLOCK):
        rows = slice(j * BLOCK, (j + 1) * BLOCK)
        if j == 0:
            kprev, vprev = kp_ref[...], vp_ref[...]
            bias = bias_ref[no_prev]
        else:
            prev_rows = slice((j - 1) * BLOCK, j * BLOCK)
            kprev, vprev = kc_ref[prev_rows, :], vc_ref[prev_rows, :]
            bias = bias_ref[0]
        kcat = jnp.concatenate([kprev, kc_ref[rows, :]], axis=0)
        vcat = jnp.concatenate([vprev, vc_ref[rows, :]], axis=0)
        for s in range(A_SLABS):
            lanes = slice(s * LANES, (s + 1) * LANES)
            num, den, mx = _attn_unit(q_ref[rows, lanes], kcat, vcat, bias)
            o_ref[rows, lanes] = (num / (den + jnp.exp2(sink_ref[s:s + 1, :] - mx))).astype(BF16)


def _attn_a(qa, ka, va, sinks, *, batch, seq, tq):
    tiles = seq // tq
    ratio = tq // BLOCK
    cur = lambda width: pl.BlockSpec((tq, width), lambda b, i: (b * tiles + i, 0))
    prev = pl.BlockSpec((BLOCK, A_KV_WIDTH),
                        lambda b, i: (jnp.maximum((b * tiles + i) * ratio - 1, 0), 0))
    return pl.pallas_call(
        functools.partial(_attn_a_kernel, tq=tq),
        grid=(batch, tiles),
        in_specs=[cur(A_WIDTH), cur(A_KV_WIDTH), prev, cur(A_KV_WIDTH), prev,
                  _resident((A_SLABS, LANES), lambda b, i: (0, 0)),
                  _resident((2, 2 * BLOCK, 2 * BLOCK), lambda b, i: (0, 0, 0))],
        out_specs=cur(A_WIDTH),
        out_shape=jax.ShapeDtypeStruct((batch * seq, A_WIDTH), BF16),
        compiler_params=_params(("parallel", "parallel")),
        name="attn_a",
    )(qa, ka, ka, va, va, sinks, _band_bias(A_WINDOW - 1))


def _attn_b_kernel(qn_ref, kn_ref, knp_ref, vn_ref, vnp_ref, q4_ref, k4_ref, k4p_ref, v4_ref, v4p_ref,
                   bias_ref, o_ref, num_ref, den_ref, mx_ref, cls_ref):
    no_prev = jnp.where(pl.program_id(1) == 0, 1, 0)
    nbr = len(B_BRANCHES)
    sub = B_TILE // B_SPLIT
    results = (num_ref, den_ref, mx_ref)

    def run(q, kprev, kcur, vprev, vcur, bias):
        return _attn_unit(q.astype(BF16), jnp.concatenate([kprev, kcur], axis=0).astype(BF16),
                          jnp.concatenate([vprev, vcur], axis=0).astype(BF16), bias)

    def merge(rows, first):
        parts = [first] + [tuple(ref[br, rows, :] for ref in results) for br in strided]
        top = functools.reduce(jnp.maximum, [p[2] for p in parts])
        es = [jnp.exp2(p[2] - top) for p in parts]
        num = functools.reduce(jnp.add, [p[0] * e for p, e in zip(parts, es)])
        den = functools.reduce(jnp.add, [p[1] * e for p, e in zip(parts, es)])
        o_ref[rows, :] = (num / den).astype(BF16)

    order = sorted(range(nbr), key=lambda br: B_BRANCHES[br][1] == 1)
    strided = [br for br in order if B_BRANCHES[br][1] != 1]
    for br in order:
        window, d = B_BRANCHES[br]
        assert window // d == BLOCK
        if d == 1:
            kprev, vprev = knp_ref[...], vnp_ref[...]
            for j in range(B_TILE // BLOCK):
                rows = slice(j * BLOCK, (j + 1) * BLOCK)
                kcur, vcur = kn_ref[rows, :], vn_ref[rows, :]
                merge(rows, run(qn_ref[rows, :], kprev, kcur, vprev, vcur,
                                bias_ref[no_prev if j == 0 else 0]))
                kprev, vprev = kcur, vcur
        elif d == B_SPLIT:
            for c in range(B_SPLIT):
                kprev, vprev = k4p_ref[c, sub - BLOCK:sub, :], v4p_ref[c, sub - BLOCK:sub, :]
                for j in range(sub // BLOCK):
                    rows = slice(j * BLOCK, (j + 1) * BLOCK)
                    kcur, vcur = k4_ref[c, rows, :], v4_ref[c, rows, :]
                    res = run(q4_ref[c, rows, :], kprev, kcur, vprev, vcur,
                              bias_ref[no_prev if j == 0 else 0])
                    for ref, val in zip(results, res):
                        ref[br, pl.ds(j * BLOCK * d + c, BLOCK, stride=d), :] = val
                    kprev, vprev = kcur, vcur
        else:
            inner = d // B_SPLIT
            assert d % B_SPLIT == 0 and sub == BLOCK * inner
            for c in range(B_SPLIT):
                for ri in range(inner):
                    rows = pl.ds(ri, BLOCK, stride=inner)
                    res = run(q4_ref[c, rows, :], k4p_ref[c, rows, :], k4_ref[c, rows, :],
                              v4p_ref[c, rows, :], v4_ref[c, rows, :], bias_ref[no_prev])
                    for a, val in enumerate(res):
                        cls_ref[a, c, rows, :] = val
            for a, ref in enumerate(results):
                for c in range(B_SPLIT):
                    ref[br, pl.ds(c, sub, stride=B_SPLIT), :] = cls_ref[a, c]


def _attn_b(qn, kn, vn, q4, k4, v4, *, batch, seq):
    tiles = seq // B_TILE
    sub = B_TILE // B_SPLIT
    tile = lambda b, i: b * tiles + i
    nat = pl.BlockSpec((None, B_TILE, LANES), lambda b, i, s: (s, tile(b, i), 0))
    halo = pl.BlockSpec((None, BLOCK, LANES),
                        lambda b, i, s: (s, jnp.maximum(tile(b, i) * (B_TILE // BLOCK) - 1, 0), 0))
    cls = pl.BlockSpec((None, None, B_SPLIT, sub, LANES), lambda b, i, s: (s, tile(b, i), 0, 0, 0))
    cls_prev = pl.BlockSpec((None, None, B_SPLIT, sub, LANES),
                            lambda b, i, s: (s, jnp.maximum(tile(b, i) - 1, 0), 0, 0, 0))
    nbr = len(B_BRANCHES)
    return pl.pallas_call(
        _attn_b_kernel,
        grid=(batch, tiles, B_SLABS),
        in_specs=[nat, nat, halo, nat, halo, cls, cls, cls_prev, cls, cls_prev,
                  _resident((2, 2 * BLOCK, 2 * BLOCK), lambda b, i, s: (0, 0, 0))],
        out_specs=nat,
        out_shape=jax.ShapeDtypeStruct((B_SLABS, batch * seq, LANES), BF16),
        scratch_shapes=[pltpu.VMEM((nbr, B_TILE, LANES), F32)] * 3
                       + [pltpu.VMEM((3, B_SPLIT, sub, LANES), F32)],
        compiler_params=_params(("parallel", "parallel", "parallel")),
        name="attn_b",
    )(qn, kn, kn, vn, vn, q4, k4, k4, v4, v4, _band_bias(BLOCK))


def _a_head_order():
    group = A_Q_HEADS // A_KV_HEADS
    return [kv * group + m for m in range(group) for kv in range(A_KV_HEADS)]


def _a_column_perm():
    cols = []
    for h in _a_head_order():
        cols.extend(range(h * HEAD_DIM, (h + 1) * HEAD_DIM))
    return jnp.array(cols, dtype=jnp.int32)


def _block_diag(w):
    depth, blocks, n, _ = w.shape
    eye = jnp.eye(blocks, dtype=w.dtype)
    return jnp.einsum('lhij,hg->lhigj', w, eye).reshape(depth, blocks * n, blocks * n)


def kernel(x, positions, norm_ffn1, ffn1_gate, ffn1_up, ffn1_down, norm_mix, w_in, attn_sinks,
           conv_w, conv_b, rg_w_r, rg_b_r, rg_w_i, rg_b_i, rg_lambda, w_out,
           norm_ffn2, ffn2_gate, ffn2_up, ffn2_down, norm_final):
    batch, seq, _ = x.shape
    depth = w_in.shape[0]
    n = batch * seq
    assert seq % B_TILE == 0
    tm = 512
    tm_ffn = 1024

    perm = _a_column_perm()
    a_end = A_WIDTH + 2 * A_KV_WIDTH
    b_end = a_end + 3 * B_WIDTH
    w_in_b = jnp.concatenate([w_in[:, :, b_end:],
                              w_in[:, :, :A_WIDTH][:, :, perm] * (SCALE * LOG2E),
                              w_in[:, :, A_WIDTH:a_end],
                              w_in[:, :, a_end:a_end + B_WIDTH] * (SCALE * LOG2E),
                              w_in[:, :, a_end + B_WIDTH:b_end]], axis=2).astype(BF16)
    w_out_b = jnp.concatenate([w_out[:, :A_WIDTH][:, perm], w_out[:, A_WIDTH:]], axis=1).astype(BF16)
    sinks = jnp.repeat(attn_sinks[:, jnp.array(_a_head_order())], HEAD_DIM, axis=1)
    sinks = sinks.reshape(depth, A_SLABS, LANES) * LOG2E
    w_gates = jnp.concatenate([_block_diag(rg_w_r), _block_diag(rg_w_i)], axis=2).astype(BF16)
    b_gates = jnp.concatenate([rg_b_r.reshape(depth, 1, C_WIDTH), rg_b_i.reshape(depth, 1, C_WIDTH)], axis=2)
    ffn_w = [(g.astype(BF16), u.astype(BF16), d.astype(BF16))
             for g, u, d in ((ffn1_gate, ffn1_up, ffn1_down), (ffn2_gate, ffn2_up, ffn2_down))]
    vec = lambda v: v.reshape(depth, 1, -1)
    gf = norm_final.reshape(1, D_MODEL)

    cos, sin, keep = _rope_tables(positions)
    h = x.reshape(n, D_MODEL)
    for l in range(depth):
        h = _ffn(h, vec(norm_ffn1), *ffn_w[0], gf, l, final=False, tm=tm_ffn)
        qa, ka, va, qb, kb, vb, q4, k4, v4, oc = _inproj(
            h, vec(norm_mix), w_in_b, cos, sin, keep, conv_w, vec(conv_b), w_gates, b_gates,
            vec(rg_lambda), l, tm=tm, seq=seq)
        oa = _attn_a(qa, ka, va, sinks[l], batch=batch, seq=seq, tq=B_TILE)
        ob = _attn_b(qb, kb, vb, q4, k4, v4, batch=batch, seq=seq)
        h = _ffn(h, vec(norm_ffn2), *ffn_w[1], gf, l, final=(l == depth - 1), tm=tm_ffn,
                 mix=(oa, ob, oc, w_out_b))
    return h.reshape(batch, seq, D_MODEL)
```

```python
import functools

import jax
import jax.numpy as jnp
from jax import lax
from jax.experimental import pallas as pl
from jax.experimental.pallas import tpu as pltpu

D_MODEL = 1024
HEAD_DIM = 64
A_Q_HEADS = 4
A_KV_HEADS = 2
A_WINDOW = 128
B_HEADS = 6
B_BRANCHES = ((128, 1), (512, 4), (2048, 16))
C_WIDTH = 384
C_BLOCKS = 6
C_CONV = 4
C_EXP = 8.0
D_FF = 2816
BLOCK = 128
ROPE_THETA = 10000.0
EPS = 1e-6
SCALE = HEAD_DIM ** -0.5
LOG2E = 1.4426950408889634

A_WIDTH = A_Q_HEADS * HEAD_DIM
A_KV_WIDTH = A_KV_HEADS * HEAD_DIM
B_WIDTH = B_HEADS * HEAD_DIM
MIX_WIDTH = A_WIDTH + B_WIDTH + C_WIDTH
IN_COLS = A_WIDTH + 2 * A_KV_WIDTH + 3 * B_WIDTH + 2 * C_WIDTH

LANES = 128
SUBLANES = 8
FF_CHUNK = 256
N_FF_CHUNKS = D_FF // FF_CHUNK
B_SLABS = B_WIDTH // LANES
A_SLABS = A_WIDTH // LANES
MAX_DIL = max(d for _, d in B_BRANCHES)
B_TILE = BLOCK * MAX_DIL
B_SPLIT = 4
VMEM_LIMIT = 56 * 1024 * 1024

F32 = jnp.float32
BF16 = jnp.bfloat16


def _params(sem):
    return pltpu.CompilerParams(dimension_semantics=sem, vmem_limit_bytes=VMEM_LIMIT)


def _resident(shape, index_map):
    return pl.BlockSpec(shape, index_map, pipeline_mode=pl.Buffered(1))


def _rms(x, g):
    return x * lax.rsqrt(jnp.mean(x * x, axis=-1, keepdims=True) + EPS) * g


def _tables_kernel(pos_ref, inv_ref, cos_ref, sin_ref, keep_ref):
    inv = inv_ref[...]
    reps = LANES // HEAD_DIM
    for r in range(pos_ref.shape[0]):
        p = pos_ref[r:r + 1, :].astype(F32)
        ang_t = inv * p
        cos_t, sin_t = jnp.cos(ang_t), jnp.sin(ang_t)
        rows = pl.ds(r * LANES, LANES)
        cos_ref[rows, :] = jnp.concatenate([cos_t, cos_t] * reps, axis=0).T
        sin_ref[rows, :] = jnp.concatenate([-sin_t, sin_t] * reps, axis=0).T
        keep_ref[rows, :] = jnp.where(jnp.broadcast_to(p, (LANES, LANES)) == 0.0, 0.0, 1.0).T


def _rope_tables(positions):
    n = positions.size
    rows = SUBLANES
    pos2 = positions.reshape(n // LANES, LANES)
    half = HEAD_DIM // 2
    inv = 1.0 / (ROPE_THETA ** (jnp.arange(0, HEAD_DIM, 2, dtype=F32) / HEAD_DIM))
    tab = jax.ShapeDtypeStruct((n, LANES), F32)
    out_spec = pl.BlockSpec((rows * LANES, LANES), lambda i: (i, 0))
    return pl.pallas_call(
        _tables_kernel,
        grid=(n // (rows * LANES),),
        in_specs=[pl.BlockSpec((rows, LANES), lambda i: (i, 0)),
                  pl.BlockSpec((half, 1), lambda i: (0, 0))],
        out_specs=[out_spec, out_spec, out_spec],
        out_shape=[tab, tab, tab],
        compiler_params=_params(("arbitrary",)),
        name="rope_tables",
    )(pos2, inv.reshape(half, 1))


def _ffn_kernel(*refs, mixed, final):
    if mixed:
        x_in_ref, a_ref, b_ref, c_ref, wo_ref, *refs, x_ref = refs
        mix = jnp.concatenate([a_ref[...]] + [b_ref[s] for s in range(B_SLABS)] + [c_ref[...]], axis=1)
        x_ref[...] = x_in_ref[...] + jnp.dot(mix, wo_ref[...], preferred_element_type=F32)
    else:
        x_ref, *refs = refs
    g_ref, wg_ref, wu_ref, wd_ref, gf_ref, o_ref, xn_ref, acc_ref = refs
    xn_ref[...] = _rms(x_ref[...], g_ref[...]).astype(BF16)
    for c in range(N_FF_CHUNKS):
        xn = xn_ref[...]
        cols = slice(c * FF_CHUNK, (c + 1) * FF_CHUNK)
        g = jnp.dot(xn, wg_ref[:, cols], preferred_element_type=F32)
        u = jnp.dot(xn, wu_ref[:, cols], preferred_element_type=F32)
        h = (g * (1.0 / (1.0 + jnp.exp(-g))) * u).astype(BF16)
        d = jnp.dot(h, wd_ref[cols, :], preferred_element_type=F32)
        if c == 0:
            acc_ref[...] = d
        elif c < N_FF_CHUNKS - 1:
            acc_ref[...] += d
        else:
            y = x_ref[...] + 0.5 * (acc_ref[...] + d)
            if final:
                y = _rms(y, gf_ref[...])
            o_ref[...] = y


def _ffn(x, g, wg, wu, wd, gf, layer, *, final, tm, mix=None):
    n = x.shape[0]
    row = lambda width: pl.BlockSpec((tm, width), lambda i: (i, 0))
    per_layer = lambda rows, cols: _resident((None, rows, cols), lambda i: (layer, 0, 0))
    in_specs = [row(D_MODEL)]
    args = [x]
    scratch = [pltpu.VMEM((tm, D_MODEL), BF16), pltpu.VMEM((tm, D_MODEL), F32)]
    if mix is not None:
        in_specs += [row(A_WIDTH), pl.BlockSpec((B_SLABS, tm, LANES), lambda i: (0, i, 0)),
                     row(C_WIDTH), per_layer(MIX_WIDTH, D_MODEL)]
        args += list(mix)
        scratch.append(pltpu.VMEM((tm, D_MODEL), F32))
    in_specs += [per_layer(1, D_MODEL), per_layer(D_MODEL, D_FF), per_layer(D_MODEL, D_FF),
                 per_layer(D_FF, D_MODEL), _resident((1, D_MODEL), lambda i: (0, 0))]
    args += [g, wg, wu, wd, gf]
    return pl.pallas_call(
        functools.partial(_ffn_kernel, mixed=mix is not None, final=final),
        grid=(n // tm,),
        in_specs=in_specs,
        out_specs=row(D_MODEL),
        out_shape=jax.ShapeDtypeStruct((n, D_MODEL), F32),
        scratch_shapes=scratch,
        compiler_params=_params(("parallel",)),
        name="ffn_mix" if mix is not None else "ffn",
    )(*args)


def _rope(v, cos, sin_signed, first_half):
    rot = jnp.where(first_half, pltpu.roll(v, LANES - HEAD_DIM // 2, axis=1),
                    pltpu.roll(v, HEAD_DIM // 2, axis=1))
    return v * cos + rot * sin_signed


def _inproj_kernel(x_ref, g_ref, w_ref, cos_ref, sin_ref, keep_ref, cw_ref, cb_ref, wgate_ref,
                   bgate_ref, lam_ref,
                   qa_ref, ka_ref, va_ref, qb_ref, kb_ref, vb_ref, q4_ref, k4_ref, v4_ref, oc_ref,
                   tmp_ref, xbuf_ref, a_ref, b_ref, cin_ref, pc_ref, *, tiles_per_seq):
    step = pl.program_id(0)

    @pl.when(step == 0)
    def _():
        pc_ref[...] = jnp.zeros_like(pc_ref)
        cin_ref[...] = jnp.zeros_like(cin_ref)
        xbuf_ref[...] = jnp.zeros_like(xbuf_ref)

    @pl.when((step + tiles_per_seq - 1) % tiles_per_seq == 0)
    def _():
        cin_ref[...] = jnp.zeros_like(cin_ref)
        xbuf_ref[0:SUBLANES, :] = jnp.zeros((SUBLANES, C_WIDTH), F32)

    xc_lag, gate_lag = pc_ref[:, :C_WIDTH], pc_ref[:, C_WIDTH:]

    xn = _rms(x_ref[...], g_ref[...]).astype(BF16)
    cos = cos_ref[...]
    sin = sin_ref[...]
    first_half = lax.broadcasted_iota(jnp.int32, cos.shape, 1) % HEAD_DIM < HEAD_DIM // 2

    group = 4 * LANES
    groups = [jnp.dot(xn, w_ref[:, c:min(c + group, IN_COLS)], preferred_element_type=F32)
              for c in range(0, IN_COLS, group)]

    def proj(col, width):
        parts = [groups[c // group][:, c % group:c % group + LANES] for c in range(col, col + width, LANES)]
        return parts[0] if len(parts) == 1 else jnp.concatenate(parts, axis=1)

    pc_ref[...] = proj(0, 2 * C_WIDTH)
    col = 2 * C_WIDTH
    for s in range(A_SLABS):
        qa_ref[:, s * LANES:(s + 1) * LANES] = _rope(proj(col, LANES), cos, sin, first_half).astype(BF16)
        col += LANES
    ka_ref[...] = _rope(proj(col, LANES), cos, sin, first_half).astype(BF16)
    col += LANES
    va_ref[...] = proj(col, LANES).astype(BF16)
    col += LANES
    sub = x_ref.shape[0] // B_SPLIT
    for t, (ref, split_ref, rotary) in enumerate(((qb_ref, q4_ref, True), (kb_ref, k4_ref, True),
                                                  (vb_ref, v4_ref, False))):
        for s in range(B_SLABS):
            v = proj(col, LANES)
            if rotary:
                v = _rope(v, cos, sin, first_half)
            ref[s] = v.astype(BF16)
            tmp_ref[t * B_SLABS + s] = v
            for c in range(B_SPLIT):
                split_ref[s, c] = tmp_ref[t * B_SLABS + s, pl.ds(c, sub, stride=B_SPLIT), :]
            col += LANES
    oc_ref[...] = _rglru_tile(xc_lag, gate_lag, keep_ref[...],
                              cw_ref, cb_ref, wgate_ref, bgate_ref, lam_ref,
                              xbuf_ref, a_ref, b_ref, cin_ref).astype(BF16)


def _rglru_tile(xc, gate, keep, cw_ref, cb_ref, wg_ref, bg_ref, lam_ref,
                xbuf_ref, a_ref, b_ref, cin_ref):
    tm = xc.shape[0]

    xbuf_ref[SUBLANES:, :] = xc
    y = cb_ref[...] + cw_ref[0:1, :] * xc
    for j in range(1, C_CONV):
        y = y + cw_ref[j:j + 1, :] * xbuf_ref[pl.ds(SUBLANES - j, tm), :]
    xbuf_ref[0:SUBLANES, :] = xbuf_ref[tm:tm + SUBLANES, :]

    gates = jnp.dot(y.astype(BF16), wg_ref[...], preferred_element_type=F32) + bg_ref[...]
    gates = 1.0 / (1.0 + jnp.exp(-gates))
    r, ig = gates[:, :C_WIDTH], gates[:, C_WIDTH:]
    lam = lam_ref[...]
    softplus = jnp.maximum(-lam, 0.0) + jnp.log(1.0 + jnp.exp(-jnp.abs(lam)))
    a = jnp.exp(-C_EXP * r * softplus)
    keep = jnp.concatenate([keep] * (C_WIDTH // LANES), axis=1) != 0.0
    gap = 1.0 - a * a
    root = jnp.where(gap > 0.0, gap * lax.rsqrt(gap), 0.0)
    mult = jnp.where(keep, root, 1.0)
    a = jnp.where(keep, a, 0.0)
    b = mult * (ig * y)

    a = a.reshape(tm // SUBLANES, SUBLANES, C_WIDTH)
    b = b.reshape(tm // SUBLANES, SUBLANES, C_WIDTH)
    in_group = lax.broadcasted_iota(jnp.int32, a.shape, 1)
    sh = 1
    while sh < SUBLANES:
        ok = in_group >= sh
        b = b + a * jnp.where(ok, pltpu.roll(b, sh, axis=1), 0.0)
        a = a * jnp.where(ok, pltpu.roll(a, sh, axis=1), 1.0)
        sh *= 2
    a = a.reshape(tm, C_WIDTH)
    b = b.reshape(tm, C_WIDTH)

    groups = tm // SUBLANES
    grp = lax.broadcasted_iota(jnp.int32, (groups, LANES), 0)
    last = SUBLANES - 1
    slabs = []
    for s in range(C_WIDTH // LANES):
        lanes = slice(s * LANES, (s + 1) * LANES)
        a_ref[s] = a[:, lanes]
        b_ref[s] = b[:, lanes]
        a_end = a_ref[s, pl.ds(last, groups, stride=SUBLANES), :]
        b_end = b_ref[s, pl.ds(last, groups, stride=SUBLANES), :]
        b_end = b_end + jnp.where(grp == 0, a_end * cin_ref[s, last:SUBLANES, :], 0.0)
        sh = 1
        while sh < groups:
            ok = grp >= sh
            b_end = b_end + a_end * jnp.where(ok, pltpu.roll(b_end, sh, axis=0), 0.0)
            a_end = a_end * jnp.where(ok, pltpu.roll(a_end, sh, axis=0), 1.0)
            sh *= 2
        cin_ref[s, SUBLANES:, :] = b_end
        hs = []
        for g in range(groups):
            rows = slice(g * SUBLANES, (g + 1) * SUBLANES)
            hs.append(b_ref[s, rows, :] + a_ref[s, rows, :] * cin_ref[s, last + g:SUBLANES + g, :])
        slabs.append(jnp.concatenate(hs, axis=0))
        cin_ref[s, last:SUBLANES, :] = b_end[groups - 1:groups, :]
    h = jnp.concatenate(slabs, axis=1)

    gelu = 0.5 * gate * (1.0 + jnp.tanh(0.7978845608028654 * (gate + 0.044715 * gate * gate * gate)))
    return h * gelu


def _inproj(x, g, w, cos, sin, keep, cw, cb, wgate, bgate, lam, layer, *, tm, seq):
    n = x.shape[0]
    tiles = n // tm
    n_slabs = C_WIDTH // LANES
    cur = lambda i: jnp.minimum(i, tiles - 1)
    lag = lambda i: jnp.maximum(i - 1, 0)
    row = lambda width: pl.BlockSpec((tm, width), lambda i: (cur(i), 0))
    lag_row = lambda width: pl.BlockSpec((tm, width), lambda i: (lag(i), 0))
    slab = pl.BlockSpec((B_SLABS, tm, LANES), lambda i: (0, cur(i), 0))
    per_tile = B_TILE // tm
    split = pl.BlockSpec((B_SLABS, None, B_SPLIT, tm // B_SPLIT, LANES),
                         lambda i: (0, cur(i) // per_tile, 0, cur(i) % per_tile, 0))
    sds = jax.ShapeDtypeStruct
    nat = sds((B_SLABS, n, LANES), BF16)
    cls = sds((B_SLABS, n // B_TILE, B_SPLIT, B_TILE // B_SPLIT, LANES), F32)
    per_layer = lambda rows, cols: _resident((None, rows, cols), lambda i: (layer, 0, 0))
    return pl.pallas_call(
        functools.partial(_inproj_kernel, tiles_per_seq=seq // tm),
        grid=(tiles + 1,),
        in_specs=[row(D_MODEL), per_layer(1, D_MODEL), per_layer(D_MODEL, IN_COLS),
                  row(LANES), row(LANES), lag_row(LANES),
                  per_layer(C_CONV, C_WIDTH), per_layer(1, C_WIDTH),
                  per_layer(C_WIDTH, 2 * C_WIDTH), per_layer(1, 2 * C_WIDTH), per_layer(1, C_WIDTH)],
        out_specs=[row(A_WIDTH), row(A_KV_WIDTH), row(A_KV_WIDTH), slab, slab, slab,
                   split, split, split, lag_row(C_WIDTH)],
        out_shape=[sds((n, A_WIDTH), BF16), sds((n, A_KV_WIDTH), BF16), sds((n, A_KV_WIDTH), BF16),
                   nat, nat, nat, cls, cls, cls, sds((n, C_WIDTH), BF16)],
        scratch_shapes=[pltpu.VMEM((3 * B_SLABS, tm, LANES), F32),
                        pltpu.VMEM((tm + SUBLANES, C_WIDTH), F32),
                        pltpu.VMEM((n_slabs, tm, LANES), F32), pltpu.VMEM((n_slabs, tm, LANES), F32),
                        pltpu.VMEM((n_slabs, SUBLANES + tm // SUBLANES, LANES), F32),
                        pltpu.VMEM((tm, 2 * C_WIDTH), F32)],
        compiler_params=_params(("arbitrary",)),
        name="inproj",
    )(x, g, w, cos, sin, keep, cw, cb, wgate, bgate, lam)


def _band_bias(max_dist):
    row = lax.broadcasted_iota(jnp.int32, (2 * BLOCK, 2 * BLOCK), 0) % BLOCK
    col = lax.broadcasted_iota(jnp.int32, (2 * BLOCK, 2 * BLOCK), 1)
    first_col = row + (BLOCK - max_dist)
    out = []
    for lo in (first_col, jnp.maximum(first_col, BLOCK)):
        out.append(jnp.where((col >= lo) & (col <= row + BLOCK), 0.0, -jnp.inf).astype(F32))
    return jnp.stack(out)


def _attn_unit(q, kcat, vcat, bias):
    lane = lax.broadcasted_iota(jnp.int32, (BLOCK, LANES), 1)
    left = lane < HEAD_DIM
    zero = jnp.zeros_like(q)
    q2 = jnp.concatenate([jnp.where(left, q, zero), jnp.where(left, zero, q)], axis=0)
    s = lax.dot_general(q2, kcat, (((1,), (1,)), ((), ())), preferred_element_type=F32) + bias
    m = jnp.max(s, axis=-1, keepdims=True)
    p = jnp.exp2(s - m).astype(BF16)
    v1 = jnp.concatenate([vcat, jnp.ones_like(vcat)], axis=1)
    r = jnp.dot(p, v1, preferred_element_type=F32)
    num = jnp.where(left, r[:BLOCK, :LANES], r[BLOCK:, :LANES])
    den = jnp.where(left, r[:BLOCK, LANES:], r[BLOCK:, LANES:])
    mx = jnp.where(left, jnp.broadcast_to(m[:BLOCK], (BLOCK, LANES)),
                   jnp.broadcast_to(m[BLOCK:], (BLOCK, LANES)))
    return num, den, mx


def _attn_a_kernel(q_ref, kc_ref, kp_ref, vc_ref, vp_ref, sink_ref, bias_ref, o_ref, *, tq):
    no_prev = jnp.where(pl.program_id(1) == 0, 1, 0)
    for j in range(tq // BLOCK):
        rows = slice(j * BLOCK, (j + 1) * BLOCK)
        if j == 0:
            kprev, vprev = kp_ref[...], vp_ref[...]
            bias = bias_ref[no_prev]
        else:
            prev_rows = slice((j - 1) * BLOCK, j * BLOCK)
            kprev, vprev = kc_ref[prev_rows, :], vc_ref[prev_rows, :]
            bias = bias_ref[0]
        kcat = jnp.concatenate([kprev, kc_ref[rows, :]], axis=0)
        vcat = jnp.concatenate([vprev, vc_ref[rows, :]], axis=0)
        for s in range(A_SLABS):
            lanes = slice(s * LANES, (s + 1) * LANES)
            num, den, mx = _attn_unit(q_ref[rows, lanes], kcat, vcat, bias)
            o_ref[rows, lanes] = (num / (den + jnp.exp2(sink_ref[s:s + 1, :] - mx))).astype(BF16)


def _attn_a(qa, ka, va, sinks, *, batch, seq, tq):
    tiles = seq // tq
    ratio = tq // BLOCK
    cur = lambda width: pl.BlockSpec((tq, width), lambda b, i: (b * tiles + i, 0))
    prev = pl.BlockSpec((BLOCK, A_KV_WIDTH),
                        lambda b, i: (jnp.maximum((b * tiles + i) * ratio - 1, 0), 0))
    return pl.pallas_call(
        functools.partial(_attn_a_kernel, tq=tq),
        grid=(batch, tiles),
        in_specs=[cur(A_WIDTH), cur(A_KV_WIDTH), prev, cur(A_KV_WIDTH), prev,
                  _resident((A_SLABS, LANES), lambda b, i: (0, 0)),
                  _resident((2, 2 * BLOCK, 2 * BLOCK), lambda b, i: (0, 0, 0))],
        out_specs=cur(A_WIDTH),
        out_shape=jax.ShapeDtypeStruct((batch * seq, A_WIDTH), BF16),
        compiler_params=_params(("parallel", "parallel")),
        name="attn_a",
    )(qa, ka, ka, va, va, sinks, _band_bias(A_WINDOW - 1))


def _attn_b_kernel(qn_ref, kn_ref, knp_ref, vn_ref, vnp_ref, q4_ref, k4_ref, k4p_ref, v4_ref, v4p_ref,
                   bias_ref, o_ref, num_ref, den_ref, mx_ref, cls_ref):
    no_prev = jnp.where(pl.program_id(1) == 0, 1, 0)
    nbr = len(B_BRANCHES)
    sub = B_TILE // B_SPLIT
    results = (num_ref, den_ref, mx_ref)

    def run(q, kprev, kcur, vprev, vcur, bias):
        return _attn_unit(q.astype(BF16), jnp.concatenate([kprev, kcur], axis=0).astype(BF16),
                          jnp.concatenate([vprev, vcur], axis=0).astype(BF16), bias)

    def merge(rows, first):
        parts = [first] + [tuple(ref[br, rows, :] for ref in results) for br in strided]
        top = functools.reduce(jnp.maximum, [p[2] for p in parts])
        es = [jnp.exp2(p[2] - top) for p in parts]
        num = functools.reduce(jnp.add, [p[0] * e for p, e in zip(parts, es)])
        den = functools.reduce(jnp.add, [p[1] * e for p, e in zip(parts, es)])
        o_ref[rows, :] = (num / den).astype(BF16)

    order = sorted(range(nbr), key=lambda br: B_BRANCHES[br][1] == 1)
    strided = [br for br in order if B_BRANCHES[br][1] != 1]
    for br in order:
        window, d = B_BRANCHES[br]
        assert window // d == BLOCK
        if d == 1:
            kprev, vprev = knp_ref[...], vnp_ref[...]
            for j in range(B_TILE // BLOCK):
                rows = slice(j * BLOCK, (j + 1) * BLOCK)
                kcur, vcur = kn_ref[rows, :], vn_ref[rows, :]
                merge(rows, run(qn_ref[rows, :], kprev, kcur, vprev, vcur,
                                bias_ref[no_prev if j == 0 else 0]))
                kprev, vprev = kcur, vcur
        elif d == B_SPLIT:
            for c in range(B_SPLIT):
                kprev, vprev = k4p_ref[c, sub - BLOCK:sub, :], v4p_ref[c, sub - BLOCK:sub, :]
                for j in range(sub // BLOCK):
                    rows = slice(j * BLOCK, (j + 1) * BLOCK)
                    kcur, vcur = k4_ref[c, rows, :], v4_ref[c, rows, :]
                    res = run(q4_ref[c, rows, :], kprev, kcur, vprev, vcur,
                              bias_ref[no_prev if j == 0 else 0])
                    for ref, val in zip(results, res):
                        ref[br, pl.ds(j * BLOCK * d + c, BLOCK, stride=d), :] = val
                    kprev, vprev = kcur, vcur
        else:
            inner = d // B_SPLIT
            assert d % B_SPLIT == 0 and sub == BLOCK * inner
            for c in range(B_SPLIT):
                for ri in range(inner):
                    rows = pl.ds(ri, BLOCK, stride=inner)
                    res = run(q4_ref[c, rows, :], k4p_ref[c, rows, :], k4_ref[c, rows, :],
                              v4p_ref[c, rows, :], v4_ref[c, rows, :], bias_ref[no_prev])
                    for a, val in enumerate(res):
                        cls_ref[a, c, rows, :] = val
            for a, ref in enumerate(results):
                for c in range(B_SPLIT):
                    ref[br, pl.ds(c, sub, stride=B_SPLIT), :] = cls_ref[a, c]


def _attn_b(qn, kn, vn, q4, k4, v4, *, batch, seq):
    tiles = seq // B_TILE
    sub = B_TILE // B_SPLIT
    tile = lambda b, i: b * tiles + i
    nat = pl.BlockSpec((None, B_TILE, LANES), lambda b, i, s: (s, tile(b, i), 0))
    halo = pl.BlockSpec((None, BLOCK, LANES),
                        lambda b, i, s: (s, jnp.maximum(tile(b, i) * (B_TILE // BLOCK) - 1, 0), 0))
    cls = pl.BlockSpec((None, None, B_SPLIT, sub, LANES), lambda b, i, s: (s, tile(b, i), 0, 0, 0))
    cls_prev = pl.BlockSpec((None, None, B_SPLIT, sub, LANES),
                            lambda b, i, s: (s, jnp.maximum(tile(b, i) - 1, 0), 0, 0, 0))
    nbr = len(B_BRANCHES)
    return pl.pallas_call(
        _attn_b_kernel,
        grid=(batch, tiles, B_SLABS),
        in_specs=[nat, nat, halo, nat, halo, cls, cls, cls_prev, cls, cls_prev,
                  _resident((2, 2 * BLOCK, 2 * BLOCK), lambda b, i, s: (0, 0, 0))],
        out_specs=nat,
        out_shape=jax.ShapeDtypeStruct((B_SLABS, batch * seq, LANES), BF16),
        scratch_shapes=[pltpu.VMEM((nbr, B_TILE, LANES), F32)] * 3
                       + [pltpu.VMEM((3, B_SPLIT, sub, LANES), F32)],
        compiler_params=_params(("parallel", "parallel", "parallel")),
        name="attn_b",
    )(qn, kn, kn, vn, vn, q4, k4, k4, v4, v4, _band_bias(BLOCK))


def _a_head_order():
    group = A_Q_HEADS // A_KV_HEADS
    return [kv * group + m for m in range(group) for kv in range(A_KV_HEADS)]


def _a_column_perm():
    cols = []
    for h in _a_head_order():
        cols.extend(range(h * HEAD_DIM, (h + 1) * HEAD_DIM))
    return jnp.array(cols, dtype=jnp.int32)


def _block_diag(w):
    depth, blocks, n, _ = w.shape
    eye = jnp.eye(blocks, dtype=w.dtype)
    return jnp.einsum('lhij,hg->lhigj', w, eye).reshape(depth, blocks * n, blocks * n)


def kernel(x, positions, norm_ffn1, ffn1_gate, ffn1_up, ffn1_down, norm_mix, w_in, attn_sinks,
           conv_w, conv_b, rg_w_r, rg_b_r, rg_w_i, rg_b_i, rg_lambda, w_out,
           norm_ffn2, ffn2_gate, ffn2_up, ffn2_down, norm_final):
    batch, seq, _ = x.shape
    depth = w_in.shape[0]
    n = batch * seq
    assert seq % B_TILE == 0
    tm = 512
    tm_ffn = 1024

    perm = _a_column_perm()
    a_end = A_WIDTH + 2 * A_KV_WIDTH
    b_end = a_end + 3 * B_WIDTH
    w_in_b = jnp.concatenate([w_in[:, :, b_end:],
                              w_in[:, :, :A_WIDTH][:, :, perm] * (SCALE * LOG2E),
                              w_in[:, :, A_WIDTH:a_end],
                              w_in[:, :, a_end:a_end + B_WIDTH] * (SCALE * LOG2E),
                              w_in[:, :, a_end + B_WIDTH:b_end]], axis=2).astype(BF16)
    w_out_b = jnp.concatenate([w_out[:, :A_WIDTH][:, perm], w_out[:, A_WIDTH:]], axis=1).astype(BF16)
    sinks = jnp.repeat(attn_sinks[:, jnp.array(_a_head_order())], HEAD_DIM, axis=1)
    sinks = sinks.reshape(depth, A_SLABS, LANES) * LOG2E
    w_gates = jnp.concatenate([_block_diag(rg_w_r), _block_diag(rg_w_i)], axis=2).astype(BF16)
    b_gates = jnp.concatenate([rg_b_r.reshape(depth, 1, C_WIDTH), rg_b_i.reshape(depth, 1, C_WIDTH)], axis=2)
    ffn_w = [(g.astype(BF16), u.astype(BF16), d.astype(BF16))
             for g, u, d in ((ffn1_gate, ffn1_up, ffn1_down), (ffn2_gate, ffn2_up, ffn2_down))]
    vec = lambda v: v.reshape(depth, 1, -1)
    gf = norm_final.reshape(1, D_MODEL)

    cos, sin, keep = _rope_tables(positions)
    h = x.reshape(n, D_MODEL)
    for l in range(depth):
        h = _ffn(h, vec(norm_ffn1), *ffn_w[0], gf, l, final=False, tm=tm_ffn)
        qa, ka, va, qb, kb, vb, q4, k4, v4, oc = _inproj(
            h, vec(norm_mix), w_in_b, cos, sin, keep, conv_w, vec(conv_b), w_gates, b_gates,
            vec(rg_lambda), l, tm=2 * tm, seq=seq)
        oa = _attn_a(qa, ka, va, sinks[l], batch=batch, seq=seq, tq=B_TILE)
        ob = _attn_b(qb, kb, vb, q4, k4, v4, batch=batch, seq=seq)
        h = _ffn(h, vec(norm_ffn2), *ffn_w[1], gf, l, final=(l == depth - 1), tm=tm_ffn,
                 mix=(oa, ob, oc, w_out_b))
    return h.reshape(batch, seq, D_MODEL)
```

```python
import functools

import jax
import jax.numpy as jnp
from jax import lax
from jax.experimental import pallas as pl
from jax.experimental.pallas import tpu as pltpu

D_MODEL = 1024
HEAD_DIM = 64
A_Q_HEADS = 4
A_KV_HEADS = 2
A_WINDOW = 128
B_HEADS = 6
B_BRANCHES = ((128, 1), (512, 4), (2048, 16))
C_WIDTH = 384
C_BLOCKS = 6
C_CONV = 4
C_EXP = 8.0
D_FF = 2816
BLOCK = 128
ROPE_THETA = 10000.0
EPS = 1e-6
SCALE = HEAD_DIM ** -0.5
LOG2E = 1.4426950408889634

A_WIDTH = A_Q_HEADS * HEAD_DIM
A_KV_WIDTH = A_KV_HEADS * HEAD_DIM
B_WIDTH = B_HEADS * HEAD_DIM
MIX_WIDTH = A_WIDTH + B_WIDTH + C_WIDTH
IN_COLS = A_WIDTH + 2 * A_KV_WIDTH + 3 * B_WIDTH + 2 * C_WIDTH

LANES = 128
SUBLANES = 8
FF_CHUNK = 256
N_FF_CHUNKS = D_FF // FF_CHUNK
B_SLABS = B_WIDTH // LANES
A_SLABS = A_WIDTH // LANES
MAX_DIL = max(d for _, d in B_BRANCHES)
B_TILE = BLOCK * MAX_DIL
B_SPLIT = 4
VMEM_LIMIT = 56 * 1024 * 1024

F32 = jnp.float32
BF16 = jnp.bfloat16


def _params(sem):
    return pltpu.CompilerParams(dimension_semantics=sem, vmem_limit_bytes=VMEM_LIMIT)


def _resident(shape, index_map):
    return pl.BlockSpec(shape, index_map, pipeline_mode=pl.Buffered(1))


def _rms(x, g):
    return x * lax.rsqrt(jnp.mean(x * x, axis=-1, keepdims=True) + EPS) * g


def _tables_kernel(pos_ref, inv_ref, cos_ref, sin_ref, keep_ref):
    inv = inv_ref[...]
    reps = LANES // HEAD_DIM
    for r in range(pos_ref.shape[0]):
        p = pos_ref[r:r + 1, :].astype(F32)
        ang_t = inv * p
        cos_t, sin_t = jnp.cos(ang_t), jnp.sin(ang_t)
        rows = pl.ds(r * LANES, LANES)
        cos_ref[rows, :] = jnp.concatenate([cos_t, cos_t] * reps, axis=0).T
        sin_ref[rows, :] = jnp.concatenate([-sin_t, sin_t] * reps, axis=0).T
        keep_ref[rows, :] = jnp.where(jnp.broadcast_to(p, (LANES, LANES)) == 0.0, 0.0, 1.0).T


def _rope_tables(positions):
    n = positions.size
    rows = SUBLANES
    pos2 = positions.reshape(n // LANES, LANES)
    half = HEAD_DIM // 2
    inv = 1.0 / (ROPE_THETA ** (jnp.arange(0, HEAD_DIM, 2, dtype=F32) / HEAD_DIM))
    tab = jax.ShapeDtypeStruct((n, LANES), F32)
    out_spec = pl.BlockSpec((rows * LANES, LANES), lambda i: (i, 0))
    return pl.pallas_call(
        _tables_kernel,
        grid=(n // (rows * LANES),),
        in_specs=[pl.BlockSpec((rows, LANES), lambda i: (i, 0)),
                  pl.BlockSpec((half, 1), lambda i: (0, 0))],
        out_specs=[out_spec, out_spec, out_spec],
        out_shape=[tab, tab, tab],
        compiler_params=_params(("arbitrary",)),
        name="rope_tables",
    )(pos2, inv.reshape(half, 1))


def _ffn_kernel(*refs, mixed, final):
    if mixed:
        x_in_ref, a_ref, b_ref, c_ref, wo_ref, *refs, x_ref = refs
        mix = jnp.concatenate([a_ref[...]] + [b_ref[s] for s in range(B_SLABS)] + [c_ref[...]], axis=1)
        x_ref[...] = x_in_ref[...] + jnp.dot(mix, wo_ref[...], preferred_element_type=F32)
    else:
        x_ref, *refs = refs
    g_ref, wg_ref, wu_ref, wd_ref, gf_ref, o_ref, xn_ref, acc_ref = refs
    xn_ref[...] = _rms(x_ref[...], g_ref[...]).astype(BF16)
    for c in range(N_FF_CHUNKS):
        xn = xn_ref[...]
        cols = slice(c * FF_CHUNK, (c + 1) * FF_CHUNK)
        g = jnp.dot(xn, wg_ref[:, cols], preferred_element_type=F32)
        u = jnp.dot(xn, wu_ref[:, cols], preferred_element_type=F32)
        h = (g * (1.0 / (1.0 + jnp.exp(-g))) * u).astype(BF16)
        d = jnp.dot(h, wd_ref[cols, :], preferred_element_type=F32)
        if c == 0:
            acc_ref[...] = d
        elif c < N_FF_CHUNKS - 1:
            acc_ref[...] += d
        else:
            y = x_ref[...] + 0.5 * (acc_ref[...] + d)
            if final:
                y = _rms(y, gf_ref[...])
            o_ref[...] = y


def _ffn(x, g, wg, wu, wd, gf, layer, *, final, tm, mix=None):
    n = x.shape[0]
    row = lambda width: pl.BlockSpec((tm, width), lambda i: (i, 0))
    per_layer = lambda rows, cols: _resident((None, rows, cols), lambda i: (layer, 0, 0))
    in_specs = [row(D_MODEL)]
    args = [x]
    scratch = [pltpu.VMEM((tm, D_MODEL), BF16), pltpu.VMEM((tm, D_MODEL), F32)]
    if mix is not None:
        in_specs += [row(A_WIDTH), pl.BlockSpec((B_SLABS, tm, LANES), lambda i: (0, i, 0)),
                     row(C_WIDTH), per_layer(MIX_WIDTH, D_MODEL)]
        args += list(mix)
        scratch.append(pltpu.VMEM((tm, D_MODEL), F32))
    in_specs += [per_layer(1, D_MODEL), per_layer(D_MODEL, D_FF), per_layer(D_MODEL, D_FF),
                 per_layer(D_FF, D_MODEL), _resident((1, D_MODEL), lambda i: (0, 0))]
    args += [g, wg, wu, wd, gf]
    return pl.pallas_call(
        functools.partial(_ffn_kernel, mixed=mix is not None, final=final),
        grid=(n // tm,),
        in_specs=in_specs,
        out_specs=row(D_MODEL),
        out_shape=jax.ShapeDtypeStruct((n, D_MODEL), F32),
        scratch_shapes=scratch,
        compiler_params=_params(("parallel",)),
        name="ffn_mix" if mix is not None else "ffn",
    )(*args)


def _rope(v, cos, sin_signed, first_half):
    rot = jnp.where(first_half, pltpu.roll(v, LANES - HEAD_DIM // 2, axis=1),
                    pltpu.roll(v, HEAD_DIM // 2, axis=1))
    return v * cos + rot * sin_signed


def _inproj_kernel(x_ref, g_ref, w_ref, cos_ref, sin_ref, keep_ref, cw_ref, cb_ref, wgate_ref,
                   bgate_ref, lam_ref,
                   qa_ref, ka_ref, va_ref, qb_ref, kb_ref, vb_ref, q4_ref, k4_ref, v4_ref, oc_ref,
                   tmp_ref, xbuf_ref, a_ref, b_ref, cin_ref, pc_ref, *, tiles_per_seq):
    step = pl.program_id(0)

    @pl.when(step == 0)
    def _():
        pc_ref[...] = jnp.zeros_like(pc_ref)
        cin_ref[...] = jnp.zeros_like(cin_ref)
        xbuf_ref[...] = jnp.zeros_like(xbuf_ref)

    @pl.when((step + tiles_per_seq - 1) % tiles_per_seq == 0)
    def _():
        cin_ref[...] = jnp.zeros_like(cin_ref)
        xbuf_ref[0:SUBLANES, :] = jnp.zeros((SUBLANES, C_WIDTH), F32)

    xc_lag, gate_lag = pc_ref[:, :C_WIDTH], pc_ref[:, C_WIDTH:]

    xn = _rms(x_ref[...], g_ref[...]).astype(BF16)
    cos = cos_ref[...]
    sin = sin_ref[...]
    first_half = lax.broadcasted_iota(jnp.int32, cos.shape, 1) % HEAD_DIM < HEAD_DIM // 2

    group = 4 * LANES
    groups = [jnp.dot(xn, w_ref[:, c:min(c + group, IN_COLS)], preferred_element_type=F32)
              for c in range(0, IN_COLS, group)]

    def proj(col, width):
        parts = [groups[c // group][:, c % group:c % group + LANES] for c in range(col, col + width, LANES)]
        return parts[0] if len(parts) == 1 else jnp.concatenate(parts, axis=1)

    pc_ref[...] = proj(0, 2 * C_WIDTH)
    col = 2 * C_WIDTH
    for s in range(A_SLABS):
        qa_ref[:, s * LANES:(s + 1) * LANES] = _rope(proj(col, LANES), cos, sin, first_half).astype(BF16)
        col += LANES
    ka_ref[...] = _rope(proj(col, LANES), cos, sin, first_half).astype(BF16)
    col += LANES
    va_ref[...] = proj(col, LANES).astype(BF16)
    col += LANES
    sub = x_ref.shape[0] // B_SPLIT
    for t, (ref, split_ref, rotary) in enumerate(((qb_ref, q4_ref, True), (kb_ref, k4_ref, True),
                                                  (vb_ref, v4_ref, False))):
        for s in range(B_SLABS):
            v = proj(col, LANES)
            if rotary:
                v = _rope(v, cos, sin, first_half)
            ref[s] = v.astype(BF16)
            tmp_ref[t * B_SLABS + s] = v
            for c in range(B_SPLIT):
                split_ref[s, c] = tmp_ref[t * B_SLABS + s, pl.ds(c, sub, stride=B_SPLIT), :]
            col += LANES
    oc_ref[...] = _rglru_tile(xc_lag, gate_lag, keep_ref[...],
                              cw_ref, cb_ref, wgate_ref, bgate_ref, lam_ref,
                              xbuf_ref, a_ref, b_ref, cin_ref).astype(BF16)


def _rglru_tile(xc, gate, keep, cw_ref, cb_ref, wg_ref, bg_ref, lam_ref,
                xbuf_ref, a_ref, b_ref, cin_ref):
    tm = xc.shape[0]

    x3 = xc.reshape(tm // SUBLANES, SUBLANES, C_WIDTH)
    row = lax.broadcasted_iota(jnp.int32, x3.shape, 1)
    tail = xbuf_ref[...]
    y = cb_ref[...] + cw_ref[0:1, :] * x3
    for j in range(1, C_CONV):
        rolled = pltpu.roll(x3, j, axis=1)
        before = jnp.concatenate([pltpu.roll(tail, j, axis=0)[None], rolled[:-1]], axis=0)
        y = y + cw_ref[j:j + 1, :] * jnp.where(row >= j, rolled, before)
    y = y.reshape(tm, C_WIDTH)
    xbuf_ref[...] = xc[tm - SUBLANES:, :]

    gates = jnp.dot(y.astype(BF16), wg_ref[...], preferred_element_type=F32) + bg_ref[...]
    gates = 1.0 / (1.0 + jnp.exp(-gates))
    r, ig = gates[:, :C_WIDTH], gates[:, C_WIDTH:]
    lam = lam_ref[...]
    softplus = jnp.maximum(-lam, 0.0) + jnp.log(1.0 + jnp.exp(-jnp.abs(lam)))
    a = jnp.exp(-C_EXP * r * softplus)
    keep = jnp.concatenate([keep] * (C_WIDTH // LANES), axis=1) != 0.0
    gap = 1.0 - a * a
    root = jnp.where(gap > 0.0, gap * lax.rsqrt(gap), 0.0)
    mult = jnp.where(keep, root, 1.0)
    a = jnp.where(keep, a, 0.0)
    b = mult * (ig * y)

    a = a.reshape(tm // SUBLANES, SUBLANES, C_WIDTH)
    b = b.reshape(tm // SUBLANES, SUBLANES, C_WIDTH)
    in_group = lax.broadcasted_iota(jnp.int32, a.shape, 1)
    sh = 1
    while sh < SUBLANES:
        ok = in_group >= sh
        b = b + a * jnp.where(ok, pltpu.roll(b, sh, axis=1), 0.0)
        a = a * jnp.where(ok, pltpu.roll(a, sh, axis=1), 1.0)
        sh *= 2
    a = a.reshape(tm, C_WIDTH)
    b = b.reshape(tm, C_WIDTH)

    groups = tm // SUBLANES
    grp = lax.broadcasted_iota(jnp.int32, (groups, LANES), 0)
    last = SUBLANES - 1
    slabs = []
    for s in range(C_WIDTH // LANES):
        lanes = slice(s * LANES, (s + 1) * LANES)
        a_ref[s] = a[:, lanes]
        b_ref[s] = b[:, lanes]
        a_end = a_ref[s, pl.ds(last, groups, stride=SUBLANES), :]
        b_end = b_ref[s, pl.ds(last, groups, stride=SUBLANES), :]
        b_end = b_end + jnp.where(grp == 0, a_end * cin_ref[s, last:SUBLANES, :], 0.0)
        sh = 1
        while sh < groups:
            ok = grp >= sh
            b_end = b_end + a_end * jnp.where(ok, pltpu.roll(b_end, sh, axis=0), 0.0)
            a_end = a_end * jnp.where(ok, pltpu.roll(a_end, sh, axis=0), 1.0)
            sh *= 2
        cin_ref[s, SUBLANES:, :] = b_end
        hs = []
        for g in range(groups):
            rows = slice(g * SUBLANES, (g + 1) * SUBLANES)
            hs.append(b_ref[s, rows, :] + a_ref[s, rows, :] * cin_ref[s, last + g:SUBLANES + g, :])
        slabs.append(jnp.concatenate(hs, axis=0))
        cin_ref[s, last:SUBLANES, :] = b_end[groups - 1:groups, :]
    h = jnp.concatenate(slabs, axis=1)

    gelu = 0.5 * gate * (1.0 + jnp.tanh(0.7978845608028654 * (gate + 0.044715 * gate * gate * gate)))
    return h * gelu


def _inproj(x, g, w, cos, sin, keep, cw, cb, wgate, bgate, lam, layer, *, tm, seq):
    n = x.shape[0]
    tiles = n // tm
    n_slabs = C_WIDTH // LANES
    cur = lambda i: jnp.minimum(i, tiles - 1)
    lag = lambda i: jnp.maximum(i - 1, 0)
    row = lambda width: pl.BlockSpec((tm, width), lambda i: (cur(i), 0))
    lag_row = lambda width: pl.BlockSpec((tm, width), lambda i: (lag(i), 0))
    slab = pl.BlockSpec((B_SLABS, tm, LANES), lambda i: (0, cur(i), 0))
    per_tile = B_TILE // tm
    split = pl.BlockSpec((B_SLABS, None, B_SPLIT, tm // B_SPLIT, LANES),
                         lambda i: (0, cur(i) // per_tile, 0, cur(i) % per_tile, 0))
    sds = jax.ShapeDtypeStruct
    nat = sds((B_SLABS, n, LANES), BF16)
    cls = sds((B_SLABS, n // B_TILE, B_SPLIT, B_TILE // B_SPLIT, LANES), F32)
    per_layer = lambda rows, cols: _resident((None, rows, cols), lambda i: (layer, 0, 0))
    return pl.pallas_call(
        functools.partial(_inproj_kernel, tiles_per_seq=seq // tm),
        grid=(tiles + 1,),
        in_specs=[row(D_MODEL), per_layer(1, D_MODEL), per_layer(D_MODEL, IN_COLS),
                  row(LANES), row(LANES), lag_row(LANES),
                  per_layer(C_CONV, C_WIDTH), per_layer(1, C_WIDTH),
                  per_layer(C_WIDTH, 2 * C_WIDTH), per_layer(1, 2 * C_WIDTH), per_layer(1, C_WIDTH)],
        out_specs=[row(A_WIDTH), row(A_KV_WIDTH), row(A_KV_WIDTH), slab, slab, slab,
                   split, split, split, lag_row(C_WIDTH)],
        out_shape=[sds((n, A_WIDTH), BF16), sds((n, A_KV_WIDTH), BF16), sds((n, A_KV_WIDTH), BF16),
                   nat, nat, nat, cls, cls, cls, sds((n, C_WIDTH), BF16)],
        scratch_shapes=[pltpu.VMEM((3 * B_SLABS, tm, LANES), F32),
                        pltpu.VMEM((SUBLANES, C_WIDTH), F32),
                        pltpu.VMEM((n_slabs, tm, LANES), F32), pltpu.VMEM((n_slabs, tm, LANES), F32),
                        pltpu.VMEM((n_slabs, SUBLANES + tm // SUBLANES, LANES), F32),
                        pltpu.VMEM((tm, 2 * C_WIDTH), F32)],
        compiler_params=_params(("arbitrary",)),
        name="inproj",
    )(x, g, w, cos, sin, keep, cw, cb, wgate, bgate, lam)


def _band_bias(max_dist):
    row = lax.broadcasted_iota(jnp.int32, (2 * BLOCK, 2 * BLOCK), 0) % BLOCK
    col = lax.broadcasted_iota(jnp.int32, (2 * BLOCK, 2 * BLOCK), 1)
    first_col = row + (BLOCK - max_dist)
    out = []
    for lo in (first_col, jnp.maximum(first_col, BLOCK)):
        out.append(jnp.where((col >= lo) & (col <= row + BLOCK), 0.0, -jnp.inf).astype(F32))
    return jnp.stack(out)


def _attn_unit(q, kcat, vcat, bias):
    lane = lax.broadcasted_iota(jnp.int32, (BLOCK, LANES), 1)
    left = lane < HEAD_DIM
    zero = jnp.zeros_like(q)
    q2 = jnp.concatenate([jnp.where(left, q, zero), jnp.where(left, zero, q)], axis=0)
    s = lax.dot_general(q2, kcat, (((1,), (1,)), ((), ())), preferred_element_type=F32) + bias
    m = jnp.max(s, axis=-1, keepdims=True)
    p = jnp.exp2(s - m).astype(BF16)
    v1 = jnp.concatenate([vcat, jnp.ones_like(vcat)], axis=1)
    r = jnp.dot(p, v1, preferred_element_type=F32)
    num = jnp.where(left, r[:BLOCK, :LANES], r[BLOCK:, :LANES])
    den = jnp.where(left, r[:BLOCK, LANES:], r[BLOCK:, LANES:])
    mx = jnp.where(left, jnp.broadcast_to(m[:BLOCK], (BLOCK, LANES)),
                   jnp.broadcast_to(m[BLOCK:], (BLOCK, LANES)))
    return num, den, mx


def _attn_a_kernel(q_ref, kc_ref, kp_ref, vc_ref, vp_ref, sink_ref, bias_ref, o_ref, *, tq):
    no_prev = jnp.where(pl.program_id(1) == 0, 1, 0)
    for j in range(tq // BLOCK):
        rows = slice(j * BLOCK, (j + 1) * BLOCK)
        if j == 0:
            kprev, vprev = kp_ref[...], vp_ref[...]
            bias = bias_ref[no_prev]
        else:
            prev_rows = slice((j - 1) * BLOCK, j * BLOCK)
            kprev, vprev = kc_ref[prev_rows, :], vc_ref[prev_rows, :]
            bias = bias_ref[0]
        kcat = jnp.concatenate([kprev, kc_ref[rows, :]], axis=0)
        vcat = jnp.concatenate([vprev, vc_ref[rows, :]], axis=0)
        for s in range(A_SLABS):
            lanes = slice(s * LANES, (s + 1) * LANES)
            num, den, mx = _attn_unit(q_ref[rows, lanes], kcat, vcat, bias)
            o_ref[rows, lanes] = (num / (den + jnp.exp2(sink_ref[s:s + 1, :] - mx))).astype(BF16)


def _attn_a(qa, ka, va, sinks, *, batch, seq, tq):
    tiles = seq // tq
    ratio = tq // BLOCK
    cur = lambda width: pl.BlockSpec((tq, width), lambda b, i: (b * tiles + i, 0))
    prev = pl.BlockSpec((BLOCK, A_KV_WIDTH),
                        lambda b, i: (jnp.maximum((b * tiles + i) * ratio - 1, 0), 0))
    return pl.pallas_call(
        functools.partial(_attn_a_kernel, tq=tq),
        grid=(batch, tiles),
        in_specs=[cur(A_WIDTH), cur(A_KV_WIDTH), prev, cur(A_KV_WIDTH), prev,
                  _resident((A_SLABS, LANES), lambda b, i: (0, 0)),
                  _resident((2, 2 * BLOCK, 2 * BLOCK), lambda b, i: (0, 0, 0))],
        out_specs=cur(A_WIDTH),
        out_shape=jax.ShapeDtypeStruct((batch * seq, A_WIDTH), BF16),
        compiler_params=_params(("parallel", "parallel")),
        name="attn_a",
    )(qa, ka, ka, va, va, sinks, _band_bias(A_WINDOW - 1))


def _attn_b_kernel(qn_ref, kn_ref, knp_ref, vn_ref, vnp_ref, q4_ref, k4_ref, k4p_ref, v4_ref, v4p_ref,
                   bias_ref, o_ref, num_ref, den_ref, mx_ref, cls_ref):
    no_prev = jnp.where(pl.program_id(1) == 0, 1, 0)
    nbr = len(B_BRANCHES)
    sub = B_TILE // B_SPLIT
    results = (num_ref, den_ref, mx_ref)

    def run(q, kprev, kcur, vprev, vcur, bias):
        return _attn_unit(q.astype(BF16), jnp.concatenate([kprev, kcur], axis=0).astype(BF16),
                          jnp.concatenate([vprev, vcur], axis=0).astype(BF16), bias)

    def merge(rows, first):
        parts = [first] + [tuple(ref[br, rows, :] for ref in results) for br in strided]
        top = functools.reduce(jnp.maximum, [p[2] for p in parts])
        es = [jnp.exp2(p[2] - top) for p in parts]
        num = functools.reduce(jnp.add, [p[0] * e for p, e in zip(parts, es)])
        den = functools.reduce(jnp.add, [p[1] * e for p, e in zip(parts, es)])
        o_ref[rows, :] = (num / den).astype(BF16)

    order = sorted(range(nbr), key=lambda br: B_BRANCHES[br][1] == 1)
    strided = [br for br in order if B_BRANCHES[br][1] != 1]
    for br in order:
        window, d = B_BRANCHES[br]
        assert window // d == BLOCK
        if d == 1:
            kprev, vprev = knp_ref[...], vnp_ref[...]
            for j in range(B_TILE // BLOCK):
                rows = slice(j * BLOCK, (j + 1) * BLOCK)
                kcur, vcur = kn_ref[rows, :], vn_ref[rows, :]
                merge(rows, run(qn_ref[rows, :], kprev, kcur, vprev, vcur,
                                bias_ref[no_prev if j == 0 else 0]))
                kprev, vprev = kcur, vcur
        elif d == B_SPLIT:
            for c in range(B_SPLIT):
                kprev, vprev = k4p_ref[c, sub - BLOCK:sub, :], v4p_ref[c, sub - BLOCK:sub, :]
                for j in range(sub // BLOCK):
                    rows = slice(j * BLOCK, (j + 1) * BLOCK)
                    kcur, vcur = k4_ref[c, rows, :], v4_ref[c, rows, :]
                    res = run(q4_ref[c, rows, :], kprev, kcur, vprev, vcur,
                              bias_ref[no_prev if j == 0 else 0])
                    for ref, val in zip(results, res):
                        ref[br, pl.ds(j * BLOCK * d + c, BLOCK, stride=d), :] = val
                    kprev, vprev = kcur, vcur
        else:
            inner = d // B_SPLIT
            assert d % B_SPLIT == 0 and sub == BLOCK * inner
            for c in range(B_SPLIT):
                for ri in range(inner):
                    rows = pl.ds(ri, BLOCK, stride=inner)
                    res = run(q4_ref[c, rows, :], k4p_ref[c, rows, :], k4_ref[c, rows, :],
                              v4p_ref[c, rows, :], v4_ref[c, rows, :], bias_ref[no_prev])
                    for a, val in enumerate(res):
                        cls_ref[a, c, rows, :] = val
            for a, ref in enumerate(results):
                for c in range(B_SPLIT):
                    ref[br, pl.ds(c, sub, stride=B_SPLIT), :] = cls_ref[a, c]


def _attn_b(qn, kn, vn, q4, k4, v4, *, batch, seq):
    tiles = seq // B_TILE
    sub = B_TILE // B_SPLIT
    tile = lambda b, i: b * tiles + i
    nat = pl.BlockSpec((None, B_TILE, LANES), lambda b, i, s: (s, tile(b, i), 0))
    halo = pl.BlockSpec((None, BLOCK, LANES),
                        lambda b, i, s: (s, jnp.maximum(tile(b, i) * (B_TILE // BLOCK) - 1, 0), 0))
    cls = pl.BlockSpec((None, None, B_SPLIT, sub, LANES), lambda b, i, s: (s, tile(b, i), 0, 0, 0))
    cls_prev = pl.BlockSpec((None, None, B_SPLIT, sub, LANES),
                            lambda b, i, s: (s, jnp.maximum(tile(b, i) - 1, 0), 0, 0, 0))
    nbr = len(B_BRANCHES)
    return pl.pallas_call(
        _attn_b_kernel,
        grid=(batch, tiles, B_SLABS),
        in_specs=[nat, nat, halo, nat, halo, cls, cls, cls_prev, cls, cls_prev,
                  _resident((2, 2 * BLOCK, 2 * BLOCK), lambda b, i, s: (0, 0, 0))],
        out_specs=nat,
        out_shape=jax.ShapeDtypeStruct((B_SLABS, batch * seq, LANES), BF16),
        scratch_shapes=[pltpu.VMEM((nbr, B_TILE, LANES), F32)] * 3
                       + [pltpu.VMEM((3, B_SPLIT, sub, LANES), F32)],
        compiler_params=_params(("parallel", "parallel", "parallel")),
        name="attn_b",
    )(qn, kn, kn, vn, vn, q4, k4, k4, v4, v4, _band_bias(BLOCK))


def _a_head_order():
    group = A_Q_HEADS // A_KV_HEADS
    return [kv * group + m for m in range(group) for kv in range(A_KV_HEADS)]


def _a_column_perm():
    cols = []
    for h in _a_head_order():
        cols.extend(range(h * HEAD_DIM, (h + 1) * HEAD_DIM))
    return jnp.array(cols, dtype=jnp.int32)


def _block_diag(w):
    depth, blocks, n, _ = w.shape
    eye = jnp.eye(blocks, dtype=w.dtype)
    return jnp.einsum('lhij,hg->lhigj', w, eye).reshape(depth, blocks * n, blocks * n)


def kernel(x, positions, norm_ffn1, ffn1_gate, ffn1_up, ffn1_down, norm_mix, w_in, attn_sinks,
           conv_w, conv_b, rg_w_r, rg_b_r, rg_w_i, rg_b_i, rg_lambda, w_out,
           norm_ffn2, ffn2_gate, ffn2_up, ffn2_down, norm_final):
    batch, seq, _ = x.shape
    depth = w_in.shape[0]
    n = batch * seq
    assert seq % B_TILE == 0
    tm = 512
    tm_ffn = 1024

    perm = _a_column_perm()
    a_end = A_WIDTH + 2 * A_KV_WIDTH
    b_end = a_end + 3 * B_WIDTH
    w_in_b = jnp.concatenate([w_in[:, :, b_end:],
                              w_in[:, :, :A_WIDTH][:, :, perm] * (SCALE * LOG2E),
                              w_in[:, :, A_WIDTH:a_end],
                              w_in[:, :, a_end:a_end + B_WIDTH] * (SCALE * LOG2E),
                              w_in[:, :, a_end + B_WIDTH:b_end]], axis=2).astype(BF16)
    w_out_b = jnp.concatenate([w_out[:, :A_WIDTH][:, perm], w_out[:, A_WIDTH:]], axis=1).astype(BF16)
    sinks = jnp.repeat(attn_sinks[:, jnp.array(_a_head_order())], HEAD_DIM, axis=1)
    sinks = sinks.reshape(depth, A_SLABS, LANES) * LOG2E
    w_gates = jnp.concatenate([_block_diag(rg_w_r), _block_diag(rg_w_i)], axis=2).astype(BF16)
    b_gates = jnp.concatenate([rg_b_r.reshape(depth, 1, C_WIDTH), rg_b_i.reshape(depth, 1, C_WIDTH)], axis=2)
    ffn_w = [(g.astype(BF16), u.astype(BF16), d.astype(BF16))
             for g, u, d in ((ffn1_gate, ffn1_up, ffn1_down), (ffn2_gate, ffn2_up, ffn2_down))]
    vec = lambda v: v.reshape(depth, 1, -1)
    gf = norm_final.reshape(1, D_MODEL)

    cos, sin, keep = _rope_tables(positions)
    h = x.reshape(n, D_MODEL)
    for l in range(depth):
        h = _ffn(h, vec(norm_ffn1), *ffn_w[0], gf, l, final=False, tm=tm_ffn)
        qa, ka, va, qb, kb, vb, q4, k4, v4, oc = _inproj(
            h, vec(norm_mix), w_in_b, cos, sin, keep, conv_w, vec(conv_b), w_gates, b_gates,
            vec(rg_lambda), l, tm=2 * tm, seq=seq)
        oa = _attn_a(qa, ka, va, sinks[l], batch=batch, seq=seq, tq=B_TILE)
        ob = _attn_b(qb, kb, vb, q4, k4, v4, batch=batch, seq=seq)
        h = _ffn(h, vec(norm_ffn2), *ffn_w[1], gf, l, final=(l == depth - 1), tm=tm_ffn,
                 mix=(oa, ob, oc, w_out_b))
    return h.reshape(batch, seq, D_MODEL)
```

```python
import functools

import jax
import jax.numpy as jnp
from jax import lax
from jax.experimental import pallas as pl
from jax.experimental.pallas import tpu as pltpu

D_MODEL = 1024
HEAD_DIM = 64
A_Q_HEADS = 4
A_KV_HEADS = 2
A_WINDOW = 128
B_HEADS = 6
B_BRANCHES = ((128, 1), (512, 4), (2048, 16))
C_WIDTH = 384
C_BLOCKS = 6
C_CONV = 4
C_EXP = 8.0
D_FF = 2816
BLOCK = 128
ROPE_THETA = 10000.0
EPS = 1e-6
SCALE = HEAD_DIM ** -0.5
LOG2E = 1.4426950408889634

A_WIDTH = A_Q_HEADS * HEAD_DIM
A_KV_WIDTH = A_KV_HEADS * HEAD_DIM
B_WIDTH = B_HEADS * HEAD_DIM
MIX_WIDTH = A_WIDTH + B_WIDTH + C_WIDTH
IN_COLS = A_WIDTH + 2 * A_KV_WIDTH + 3 * B_WIDTH + 2 * C_WIDTH

LANES = 128
SUBLANES = 8
FF_CHUNK = 256
N_FF_CHUNKS = D_FF // FF_CHUNK
B_SLABS = B_WIDTH // LANES
A_SLABS = A_WIDTH // LANES
MAX_DIL = max(d for _, d in B_BRANCHES)
B_TILE = BLOCK * MAX_DIL
B_SPLIT = 4
VMEM_LIMIT = 56 * 1024 * 1024

F32 = jnp.float32
BF16 = jnp.bfloat16


def _params(sem):
    return pltpu.CompilerParams(dimension_semantics=sem, vmem_limit_bytes=VMEM_LIMIT)


def _resident(shape, index_map):
    return pl.BlockSpec(shape, index_map, pipeline_mode=pl.Buffered(1))


def _rms(x, g):
    return x * lax.rsqrt(jnp.mean(x * x, axis=-1, keepdims=True) + EPS) * g


def _tables_kernel(pos_ref, inv_ref, cos_ref, sin_ref, keep_ref):
    inv = inv_ref[...]
    reps = LANES // HEAD_DIM
    for r in range(pos_ref.shape[0]):
        p = pos_ref[r:r + 1, :].astype(F32)
        ang_t = inv * p
        cos_t, sin_t = jnp.cos(ang_t), jnp.sin(ang_t)
        rows = pl.ds(r * LANES, LANES)
        cos_ref[rows, :] = jnp.concatenate([cos_t, cos_t] * reps, axis=0).T
        sin_ref[rows, :] = jnp.concatenate([-sin_t, sin_t] * reps, axis=0).T
        keep_ref[rows, :] = jnp.where(jnp.broadcast_to(p, (LANES, LANES)) == 0.0, 0.0, 1.0).T


def _rope_tables(positions):
    n = positions.size
    rows = 4 * SUBLANES
    pos2 = positions.reshape(n // LANES, LANES)
    half = HEAD_DIM // 2
    inv = 1.0 / (ROPE_THETA ** (jnp.arange(0, HEAD_DIM, 2, dtype=F32) / HEAD_DIM))
    tab = jax.ShapeDtypeStruct((n, LANES), F32)
    out_spec = pl.BlockSpec((rows * LANES, LANES), lambda i: (i, 0))
    return pl.pallas_call(
        _tables_kernel,
        grid=(n // (rows * LANES),),
        in_specs=[pl.BlockSpec((rows, LANES), lambda i: (i, 0)),
                  pl.BlockSpec((half, 1), lambda i: (0, 0))],
        out_specs=[out_spec, out_spec, out_spec],
        out_shape=[tab, tab, tab],
        compiler_params=_params(("arbitrary",)),
        name="rope_tables",
    )(pos2, inv.reshape(half, 1))


def _ffn_kernel(*refs, mixed, final):
    if mixed:
        x_in_ref, a_ref, b_ref, c_ref, wo_ref, *refs, x_ref = refs
        mix = jnp.concatenate([a_ref[...]] + [b_ref[s] for s in range(B_SLABS)] + [c_ref[...]], axis=1)
        x_ref[...] = x_in_ref[...] + jnp.dot(mix, wo_ref[...], preferred_element_type=F32)
    else:
        x_ref, *refs = refs
    g_ref, wg_ref, wu_ref, wd_ref, gf_ref, o_ref, xn_ref, acc_ref = refs
    xn_ref[...] = _rms(x_ref[...], g_ref[...]).astype(BF16)
    for c in range(N_FF_CHUNKS):
        xn = xn_ref[...]
        cols = slice(c * FF_CHUNK, (c + 1) * FF_CHUNK)
        g = jnp.dot(xn, wg_ref[:, cols], preferred_element_type=F32)
        u = jnp.dot(xn, wu_ref[:, cols], preferred_element_type=F32)
        h = (g * (1.0 / (1.0 + jnp.exp(-g))) * u).astype(BF16)
        d = jnp.dot(h, wd_ref[cols, :], preferred_element_type=F32)
        if c == 0:
            acc_ref[...] = d
        elif c < N_FF_CHUNKS - 1:
            acc_ref[...] += d
        else:
            y = x_ref[...] + 0.5 * (acc_ref[...] + d)
            if final:
                y = _rms(y, gf_ref[...])
            o_ref[...] = y


def _ffn(x, g, wg, wu, wd, gf, layer, *, final, tm, mix=None):
    n = x.shape[0]
    row = lambda width: pl.BlockSpec((tm, width), lambda i: (i, 0))
    per_layer = lambda rows, cols: _resident((None, rows, cols), lambda i: (layer, 0, 0))
    in_specs = [row(D_MODEL)]
    args = [x]
    scratch = [pltpu.VMEM((tm, D_MODEL), BF16), pltpu.VMEM((tm, D_MODEL), F32)]
    if mix is not None:
        in_specs += [row(A_WIDTH), pl.BlockSpec((B_SLABS, tm, LANES), lambda i: (0, i, 0)),
                     row(C_WIDTH), per_layer(MIX_WIDTH, D_MODEL)]
        args += list(mix)
        scratch.append(pltpu.VMEM((tm, D_MODEL), F32))
    in_specs += [per_layer(1, D_MODEL), per_layer(D_MODEL, D_FF), per_layer(D_MODEL, D_FF),
                 per_layer(D_FF, D_MODEL), _resident((1, D_MODEL), lambda i: (0, 0))]
    args += [g, wg, wu, wd, gf]
    return pl.pallas_call(
        functools.partial(_ffn_kernel, mixed=mix is not None, final=final),
        grid=(n // tm,),
        in_specs=in_specs,
        out_specs=row(D_MODEL),
        out_shape=jax.ShapeDtypeStruct((n, D_MODEL), F32),
        scratch_shapes=scratch,
        compiler_params=_params(("parallel",)),
        name="ffn_mix" if mix is not None else "ffn",
    )(*args)


def _rope(v, cos, sin_signed, first_half):
    rot = jnp.where(first_half, pltpu.roll(v, LANES - HEAD_DIM // 2, axis=1),
                    pltpu.roll(v, HEAD_DIM // 2, axis=1))
    return v * cos + rot * sin_signed


def _inproj_kernel(x_ref, g_ref, w_ref, cos_ref, sin_ref, keep_ref, cw_ref, cb_ref, wgate_ref,
                   bgate_ref, lam_ref,
                   qa_ref, ka_ref, va_ref, qb_ref, kb_ref, vb_ref, q4_ref, k4_ref, v4_ref, oc_ref,
                   tmp_ref, xbuf_ref, a_ref, b_ref, cin_ref, pc_ref, *, tiles_per_seq):
    step = pl.program_id(0)

    @pl.when(step == 0)
    def _():
        pc_ref[...] = jnp.zeros_like(pc_ref)
        cin_ref[...] = jnp.zeros_like(cin_ref)
        xbuf_ref[...] = jnp.zeros_like(xbuf_ref)

    @pl.when((step + tiles_per_seq - 1) % tiles_per_seq == 0)
    def _():
        cin_ref[...] = jnp.zeros_like(cin_ref)
        xbuf_ref[0:SUBLANES, :] = jnp.zeros((SUBLANES, C_WIDTH), F32)

    xc_lag, gate_lag = pc_ref[:, :C_WIDTH], pc_ref[:, C_WIDTH:]

    xn = _rms(x_ref[...], g_ref[...]).astype(BF16)
    cos = cos_ref[...]
    sin = sin_ref[...]
    first_half = lax.broadcasted_iota(jnp.int32, cos.shape, 1) % HEAD_DIM < HEAD_DIM // 2

    group = 4 * LANES
    groups = [jnp.dot(xn, w_ref[:, c:min(c + group, IN_COLS)], preferred_element_type=F32)
              for c in range(0, IN_COLS, group)]

    def proj(col, width):
        parts = [groups[c // group][:, c % group:c % group + LANES] for c in range(col, col + width, LANES)]
        return parts[0] if len(parts) == 1 else jnp.concatenate(parts, axis=1)

    pc_ref[...] = proj(0, 2 * C_WIDTH)
    col = 2 * C_WIDTH
    for s in range(A_SLABS):
        qa_ref[:, s * LANES:(s + 1) * LANES] = _rope(proj(col, LANES), cos, sin, first_half).astype(BF16)
        col += LANES
    ka_ref[...] = _rope(proj(col, LANES), cos, sin, first_half).astype(BF16)
    col += LANES
    va_ref[...] = proj(col, LANES).astype(BF16)
    col += LANES
    sub = x_ref.shape[0] // B_SPLIT
    for t, (ref, split_ref, rotary) in enumerate(((qb_ref, q4_ref, True), (kb_ref, k4_ref, True),
                                                  (vb_ref, v4_ref, False))):
        for s in range(B_SLABS):
            v = proj(col, LANES)
            if rotary:
                v = _rope(v, cos, sin, first_half)
            ref[s] = v.astype(BF16)
            tmp_ref[t * B_SLABS + s] = v
            for c in range(B_SPLIT):
                split_ref[s, c] = tmp_ref[t * B_SLABS + s, pl.ds(c, sub, stride=B_SPLIT), :]
            col += LANES
    oc_ref[...] = _rglru_tile(xc_lag, gate_lag, keep_ref[...],
                              cw_ref, cb_ref, wgate_ref, bgate_ref, lam_ref,
                              xbuf_ref, a_ref, b_ref, cin_ref).astype(BF16)


def _rglru_tile(xc, gate, keep, cw_ref, cb_ref, wg_ref, bg_ref, lam_ref,
                xbuf_ref, a_ref, b_ref, cin_ref):
    tm = xc.shape[0]

    xbuf_ref[SUBLANES:, :] = xc
    y = cb_ref[...] + cw_ref[0:1, :] * xc
    for j in range(1, C_CONV):
        y = y + cw_ref[j:j + 1, :] * xbuf_ref[pl.ds(SUBLANES - j, tm), :]
    xbuf_ref[0:SUBLANES, :] = xbuf_ref[tm:tm + SUBLANES, :]

    gates = jnp.dot(y.astype(BF16), wg_ref[...], preferred_element_type=F32) + bg_ref[...]
    gates = 1.0 / (1.0 + jnp.exp(-gates))
    r, ig = gates[:, :C_WIDTH], gates[:, C_WIDTH:]
    lam = lam_ref[...]
    softplus = jnp.maximum(-lam, 0.0) + jnp.log(1.0 + jnp.exp(-jnp.abs(lam)))
    a = jnp.exp(-C_EXP * r * softplus)
    keep = jnp.concatenate([keep] * (C_WIDTH // LANES), axis=1) != 0.0
    gap = 1.0 - a * a
    root = jnp.where(gap > 0.0, gap * lax.rsqrt(gap), 0.0)
    mult = jnp.where(keep, root, 1.0)
    a = jnp.where(keep, a, 0.0)
    b = mult * (ig * y)

    a = a.reshape(tm // SUBLANES, SUBLANES, C_WIDTH)
    b = b.reshape(tm // SUBLANES, SUBLANES, C_WIDTH)
    in_group = lax.broadcasted_iota(jnp.int32, a.shape, 1)
    sh = 1
    while sh < SUBLANES:
        ok = in_group >= sh
        b = b + a * jnp.where(ok, pltpu.roll(b, sh, axis=1), 0.0)
        a = a * jnp.where(ok, pltpu.roll(a, sh, axis=1), 1.0)
        sh *= 2
    a = a.reshape(tm, C_WIDTH)
    b = b.reshape(tm, C_WIDTH)

    groups = tm // SUBLANES
    grp = lax.broadcasted_iota(jnp.int32, (groups, LANES), 0)
    last = SUBLANES - 1
    slabs = []
    for s in range(C_WIDTH // LANES):
        lanes = slice(s * LANES, (s + 1) * LANES)
        a_ref[s] = a[:, lanes]
        b_ref[s] = b[:, lanes]
        a_end = a_ref[s, pl.ds(last, groups, stride=SUBLANES), :]
        b_end = b_ref[s, pl.ds(last, groups, stride=SUBLANES), :]
        b_end = b_end + jnp.where(grp == 0, a_end * cin_ref[s, last:SUBLANES, :], 0.0)
        sh = 1
        while sh < groups:
            ok = grp >= sh
            b_end = b_end + a_end * jnp.where(ok, pltpu.roll(b_end, sh, axis=0), 0.0)
            a_end = a_end * jnp.where(ok, pltpu.roll(a_end, sh, axis=0), 1.0)
            sh *= 2
        cin_ref[s, SUBLANES:, :] = b_end
        hs = []
        for g in range(groups):
            rows = slice(g * SUBLANES, (g + 1) * SUBLANES)
            hs.append(b_ref[s, rows, :] + a_ref[s, rows, :] * cin_ref[s, last + g:SUBLANES + g, :])
        slabs.append(jnp.concatenate(hs, axis=0))
        cin_ref[s, last:SUBLANES, :] = b_end[groups - 1:groups, :]
    h = jnp.concatenate(slabs, axis=1)

    gelu = 0.5 * gate * (1.0 + jnp.tanh(0.7978845608028654 * (gate + 0.044715 * gate * gate * gate)))
    return h * gelu


def _inproj(x, g, w, cos, sin, keep, cw, cb, wgate, bgate, lam, layer, *, tm, seq):
    n = x.shape[0]
    tiles = n // tm
    n_slabs = C_WIDTH // LANES
    cur = lambda i: jnp.minimum(i, tiles - 1)
    lag = lambda i: jnp.maximum(i - 1, 0)
    row = lambda width: pl.BlockSpec((tm, width), lambda i: (cur(i), 0))
    lag_row = lambda width: pl.BlockSpec((tm, width), lambda i: (lag(i), 0))
    slab = pl.BlockSpec((B_SLABS, tm, LANES), lambda i: (0, cur(i), 0))
    per_tile = B_TILE // tm
    split = pl.BlockSpec((B_SLABS, None, B_SPLIT, tm // B_SPLIT, LANES),
                         lambda i: (0, cur(i) // per_tile, 0, cur(i) % per_tile, 0))
    sds = jax.ShapeDtypeStruct
    nat = sds((B_SLABS, n, LANES), BF16)
    cls = sds((B_SLABS, n // B_TILE, B_SPLIT, B_TILE // B_SPLIT, LANES), F32)
    per_layer = lambda rows, cols: _resident((None, rows, cols), lambda i: (layer, 0, 0))
    return pl.pallas_call(
        functools.partial(_inproj_kernel, tiles_per_seq=seq // tm),
        grid=(tiles + 1,),
        in_specs=[row(D_MODEL), per_layer(1, D_MODEL), per_layer(D_MODEL, IN_COLS),
                  row(LANES), row(LANES), lag_row(LANES),
                  per_layer(C_CONV, C_WIDTH), per_layer(1, C_WIDTH),
                  per_layer(C_WIDTH, 2 * C_WIDTH), per_layer(1, 2 * C_WIDTH), per_layer(1, C_WIDTH)],
        out_specs=[row(A_WIDTH), row(A_KV_WIDTH), row(A_KV_WIDTH), slab, slab, slab,
                   split, split, split, lag_row(C_WIDTH)],
        out_shape=[sds((n, A_WIDTH), BF16), sds((n, A_KV_WIDTH), BF16), sds((n, A_KV_WIDTH), BF16),
                   nat, nat, nat, cls, cls, cls, sds((n, C_WIDTH), BF16)],
        scratch_shapes=[pltpu.VMEM((3 * B_SLABS, tm, LANES), F32),
                        pltpu.VMEM((tm + SUBLANES, C_WIDTH), F32),
                        pltpu.VMEM((n_slabs, tm, LANES), F32), pltpu.VMEM((n_slabs, tm, LANES), F32),
                        pltpu.VMEM((n_slabs, SUBLANES + tm // SUBLANES, LANES), F32),
                        pltpu.VMEM((tm, 2 * C_WIDTH), F32)],
        compiler_params=_params(("arbitrary",)),
        name="inproj",
    )(x, g, w, cos, sin, keep, cw, cb, wgate, bgate, lam)


def _band_bias(max_dist):
    row = lax.broadcasted_iota(jnp.int32, (2 * BLOCK, 2 * BLOCK), 0) % BLOCK
    col = lax.broadcasted_iota(jnp.int32, (2 * BLOCK, 2 * BLOCK), 1)
    first_col = row + (BLOCK - max_dist)
    out = []
    for lo in (first_col, jnp.maximum(first_col, BLOCK)):
        out.append(jnp.where((col >= lo) & (col <= row + BLOCK), 0.0, -jnp.inf).astype(F32))
    return jnp.stack(out)


def _attn_unit(q, kcat, vcat, bias):
    lane = lax.broadcasted_iota(jnp.int32, (BLOCK, LANES), 1)
    left = lane < HEAD_DIM
    zero = jnp.zeros_like(q)
    q2 = jnp.concatenate([jnp.where(left, q, zero), jnp.where(left, zero, q)], axis=0)
    s = lax.dot_general(q2, kcat, (((1,), (1,)), ((), ())), preferred_element_type=F32) + bias
    m = jnp.max(s, axis=-1, keepdims=True)
    p = jnp.exp2(s - m).astype(BF16)
    v1 = jnp.concatenate([vcat, jnp.ones_like(vcat)], axis=1)
    r = jnp.dot(p, v1, preferred_element_type=F32)
    num = jnp.where(left, r[:BLOCK, :LANES], r[BLOCK:, :LANES])
    den = jnp.where(left, r[:BLOCK, LANES:], r[BLOCK:, LANES:])
    mx = jnp.where(left, jnp.broadcast_to(m[:BLOCK], (BLOCK, LANES)),
                   jnp.broadcast_to(m[BLOCK:], (BLOCK, LANES)))
    return num, den, mx


def _attn_a_kernel(q_ref, kc_ref, kp_ref, vc_ref, vp_ref, sink_ref, bias_ref, o_ref, *, tq):
    no_prev = jnp.where(pl.program_id(1) == 0, 1, 0)
    for j in range(tq // BLOCK):
        rows = slice(j * BLOCK, (j + 1) * BLOCK)
        if j == 0:
            kprev, vprev = kp_ref[...], vp_ref[...]
            bias = bias_ref[no_prev]
        else:
            prev_rows = slice((j - 1) * BLOCK, j * BLOCK)
            kprev, vprev = kc_ref[prev_rows, :], vc_ref[prev_rows, :]
            bias = bias_ref[0]
        kcat = jnp.concatenate([kprev, kc_ref[rows, :]], axis=0)
        vcat = jnp.concatenate([vprev, vc_ref[rows, :]], axis=0)
        for s in range(A_SLABS):
            lanes = slice(s * LANES, (s + 1) * LANES)
            num, den, mx = _attn_unit(q_ref[rows, lanes], kcat, vcat, bias)
            o_ref[rows, lanes] = (num / (den + jnp.exp2(sink_ref[s:s + 1, :] - mx))).astype(BF16)


def _attn_a(qa, ka, va, sinks, *, batch, seq, tq):
    tiles = seq // tq
    ratio = tq // BLOCK
    cur = lambda width: pl.BlockSpec((tq, width), lambda b, i: (b * tiles + i, 0))
    prev = pl.BlockSpec((BLOCK, A_KV_WIDTH),
                        lambda b, i: (jnp.maximum((b * tiles + i) * ratio - 1, 0), 0))
    return pl.pallas_call(
        functools.partial(_attn_a_kernel, tq=tq),
        grid=(batch, tiles),
        in_specs=[cur(A_WIDTH), cur(A_KV_WIDTH), prev, cur(A_KV_WIDTH), prev,
                  _resident((A_SLABS, LANES), lambda b, i: (0, 0)),
                  _resident((2, 2 * BLOCK, 2 * BLOCK), lambda b, i: (0, 0, 0))],
        out_specs=cur(A_WIDTH),
        out_shape=jax.ShapeDtypeStruct((batch * seq, A_WIDTH), BF16),
        compiler_params=_params(("parallel", "parallel")),
        name="attn_a",
    )(qa, ka, ka, va, va, sinks, _band_bias(A_WINDOW - 1))


def _attn_b_kernel(qn_ref, kn_ref, knp_ref, vn_ref, vnp_ref, q4_ref, k4_ref, k4p_ref, v4_ref, v4p_ref,
                   bias_ref, o_ref, num_ref, den_ref, mx_ref, cls_ref):
    no_prev = jnp.where(pl.program_id(1) == 0, 1, 0)
    nbr = len(B_BRANCHES)
    sub = B_TILE // B_SPLIT
    results = (num_ref, den_ref, mx_ref)

    def run(q, kprev, kcur, vprev, vcur, bias):
        return _attn_unit(q.astype(BF16), jnp.concatenate([kprev, kcur], axis=0).astype(BF16),
                          jnp.concatenate([vprev, vcur], axis=0).astype(BF16), bias)

    def merge(rows, first):
        parts = [first] + [tuple(ref[br, rows, :] for ref in results) for br in strided]
        top = functools.reduce(jnp.maximum, [p[2] for p in parts])
        es = [jnp.exp2(p[2] - top) for p in parts]
        num = functools.reduce(jnp.add, [p[0] * e for p, e in zip(parts, es)])
        den = functools.reduce(jnp.add, [p[1] * e for p, e in zip(parts, es)])
        o_ref[rows, :] = (num / den).astype(BF16)

    order = sorted(range(nbr), key=lambda br: B_BRANCHES[br][1] == 1)
    strided = [br for br in order if B_BRANCHES[br][1] != 1]
    for br in order:
        window, d = B_BRANCHES[br]
        assert window // d == BLOCK
        if d == 1:
            kprev, vprev = knp_ref[...], vnp_ref[...]
            for j in range(B_TILE // BLOCK):
                rows = slice(j * BLOCK, (j + 1) * BLOCK)
                kcur, vcur = kn_ref[rows, :], vn_ref[rows, :]
                merge(rows, run(qn_ref[rows, :], kprev, kcur, vprev, vcur,
                                bias_ref[no_prev if j == 0 else 0]))
                kprev, vprev = kcur, vcur
        elif d == B_SPLIT:
            for c in range(B_SPLIT):
                kprev, vprev = k4p_ref[c, sub - BLOCK:sub, :], v4p_ref[c, sub - BLOCK:sub, :]
                for j in range(sub // BLOCK):
                    rows = slice(j * BLOCK, (j + 1) * BLOCK)
                    kcur, vcur = k4_ref[c, rows, :], v4_ref[c, rows, :]
                    res = run(q4_ref[c, rows, :], kprev, kcur, vprev, vcur,
                              bias_ref[no_prev if j == 0 else 0])
                    for ref, val in zip(results, res):
                        ref[br, pl.ds(j * BLOCK * d + c, BLOCK, stride=d), :] = val
                    kprev, vprev = kcur, vcur
        else:
            inner = d // B_SPLIT
            assert d % B_SPLIT == 0 and sub == BLOCK * inner
            for c in range(B_SPLIT):
                for ri in range(inner):
                    rows = pl.ds(ri, BLOCK, stride=inner)
                    res = run(q4_ref[c, rows, :], k4p_ref[c, rows, :], k4_ref[c, rows, :],
                              v4p_ref[c, rows, :], v4_ref[c, rows, :], bias_ref[no_prev])
                    for a, val in enumerate(res):
                        cls_ref[a, c, rows, :] = val
            for a, ref in enumerate(results):
                for c in range(B_SPLIT):
                    ref[br, pl.ds(c, sub, stride=B_SPLIT), :] = cls_ref[a, c]


def _attn_b(qn, kn, vn, q4, k4, v4, *, batch, seq):
    tiles = seq // B_TILE
    sub = B_TILE // B_SPLIT
    tile = lambda b, i: b * tiles + i
    nat = pl.BlockSpec((None, B_TILE, LANES), lambda b, i, s: (s, tile(b, i), 0))
    halo = pl.BlockSpec((None, BLOCK, LANES),
                        lambda b, i, s: (s, jnp.maximum(tile(b, i) * (B_TILE // BLOCK) - 1, 0), 0))
    cls = pl.BlockSpec((None, None, B_SPLIT, sub, LANES), lambda b, i, s: (s, tile(b, i), 0, 0, 0))
    cls_prev = pl.BlockSpec((None, None, B_SPLIT, sub, LANES),
                            lambda b, i, s: (s, jnp.maximum(tile(b, i) - 1, 0), 0, 0, 0))
    nbr = len(B_BRANCHES)
    return pl.pallas_call(
        _attn_b_kernel,
        grid=(batch, tiles, B_SLABS),
        in_specs=[nat, nat, halo, nat, halo, cls, cls, cls_prev, cls, cls_prev,
                  _resident((2, 2 * BLOCK, 2 * BLOCK), lambda b, i, s: (0, 0, 0))],
        out_specs=nat,
        out_shape=jax.ShapeDtypeStruct((B_SLABS, batch * seq, LANES), BF16),
        scratch_shapes=[pltpu.VMEM((nbr, B_TILE, LANES), F32)] * 3
                       + [pltpu.VMEM((3, B_SPLIT, sub, LANES), F32)],
        compiler_params=_params(("parallel", "parallel", "parallel")),
        name="attn_b",
    )(qn, kn, kn, vn, vn, q4, k4, k4, v4, v4, _band_bias(BLOCK))


def _a_head_order():
    group = A_Q_HEADS // A_KV_HEADS
    return [kv * group + m for m in range(group) for kv in range(A_KV_HEADS)]


def _a_column_perm():
    cols = []
    for h in _a_head_order():
        cols.extend(range(h * HEAD_DIM, (h + 1) * HEAD_DIM))
    return jnp.array(cols, dtype=jnp.int32)


def _block_diag(w):
    depth, blocks, n, _ = w.shape
    eye = jnp.eye(blocks, dtype=w.dtype)
    return jnp.einsum('lhij,hg->lhigj', w, eye).reshape(depth, blocks * n, blocks * n)


def kernel(x, positions, norm_ffn1, ffn1_gate, ffn1_up, ffn1_down, norm_mix, w_in, attn_sinks,
           conv_w, conv_b, rg_w_r, rg_b_r, rg_w_i, rg_b_i, rg_lambda, w_out,
           norm_ffn2, ffn2_gate, ffn2_up, ffn2_down, norm_final):
    batch, seq, _ = x.shape
    depth = w_in.shape[0]
    n = batch * seq
    assert seq % B_TILE == 0
    tm = 512
    tm_ffn = 1024

    perm = _a_column_perm()
    a_end = A_WIDTH + 2 * A_KV_WIDTH
    b_end = a_end + 3 * B_WIDTH
    w_in_b = jnp.concatenate([w_in[:, :, b_end:],
                              w_in[:, :, :A_WIDTH][:, :, perm] * (SCALE * LOG2E),
                              w_in[:, :, A_WIDTH:a_end],
                              w_in[:, :, a_end:a_end + B_WIDTH] * (SCALE * LOG2E),
                              w_in[:, :, a_end + B_WIDTH:b_end]], axis=2).astype(BF16)
    w_out_b = jnp.concatenate([w_out[:, :A_WIDTH][:, perm], w_out[:, A_WIDTH:]], axis=1).astype(BF16)
    sinks = jnp.repeat(attn_sinks[:, jnp.array(_a_head_order())], HEAD_DIM, axis=1)
    sinks = sinks.reshape(depth, A_SLABS, LANES) * LOG2E
    w_gates = jnp.concatenate([_block_diag(rg_w_r), _block_diag(rg_w_i)], axis=2).astype(BF16)
    b_gates = jnp.concatenate([rg_b_r.reshape(depth, 1, C_WIDTH), rg_b_i.reshape(depth, 1, C_WIDTH)], axis=2)
    ffn_w = [(g.astype(BF16), u.astype(BF16), d.astype(BF16))
             for g, u, d in ((ffn1_gate, ffn1_up, ffn1_down), (ffn2_gate, ffn2_up, ffn2_down))]
    vec = lambda v: v.reshape(depth, 1, -1)
    gf = norm_final.reshape(1, D_MODEL)

    cos, sin, keep = _rope_tables(positions)
    h = x.reshape(n, D_MODEL)
    for l in range(depth):
        h = _ffn(h, vec(norm_ffn1), *ffn_w[0], gf, l, final=False, tm=tm_ffn)
        qa, ka, va, qb, kb, vb, q4, k4, v4, oc = _inproj(
            h, vec(norm_mix), w_in_b, cos, sin, keep, conv_w, vec(conv_b), w_gates, b_gates,
            vec(rg_lambda), l, tm=2 * tm, seq=seq)
        oa = _attn_a(qa, ka, va, sinks[l], batch=batch, seq=seq, tq=2 * B_TILE)
        ob = _attn_b(qb, kb, vb, q4, k4, v4, batch=batch, seq=seq)
        h = _ffn(h, vec(norm_ffn2), *ffn_w[1], gf, l, final=(l == depth - 1), tm=tm_ffn,
                 mix=(oa, ob, oc, w_out_b))
    return h.reshape(batch, seq, D_MODEL)
```
